```python
import math
import jax
import jax.numpy as jnp
from jax import lax
import numpy as np

D_MODEL = 4096
BATCH = 4
SEQ = 2048
DEPTH = 1
DEC_BATCH = 128
DEC_SEQ = 8
PAST_LEN = 16384
PAGE_SIZE = 128

N_META = 16
RMS_EPS = 1e-6
MIX_DIM = D_MODEL
GDN_DK = 128
GDN_DV = 128
GDN_HV = (MIX_DIM // 2) // GDN_DV
GDN_HK = GDN_HV // 2
GDN_KEY_DIM = GDN_HK * GDN_DK
GDN_VAL_DIM = GDN_HV * GDN_DV
GDN_CONV_DIM = 2 * GDN_KEY_DIM + GDN_VAL_DIM
CONV_W = 4
GDN_CHUNK = 64
ML_DQK = 256
ML_DV = 512
ML_H = (MIX_DIM - GDN_VAL_DIM) // ML_DV
ML_QK_DIM = ML_H * ML_DQK
ML_V_DIM = ML_H * ML_DV
ML_CHUNK = 64
GATE_SOFTCAP = 15.0
FGATE_BIAS_INIT = 3.0
IN_SPLITS = (GDN_KEY_DIM, GDN_KEY_DIM, GDN_VAL_DIM, GDN_VAL_DIM, GDN_HV, GDN_HV,
             ML_QK_DIM, ML_QK_DIM, ML_V_DIM, ML_V_DIM, ML_H, ML_H)
IN_DIM = 2 * GDN_KEY_DIM + 2 * GDN_VAL_DIM + 2 * GDN_HV + 2 * ML_QK_DIM + 2 * ML_V_DIM + 2 * ML_H
N_EXPERTS = 32
TOP_K = 4
D_FF = D_MODEL
SWIGLU_LIMIT = 7.0
SWIGLU_ALPHA = 1.702
MOE_BLOCK = 256

kernel_name = 'hymba_gdn_mlstm_moe_step'


def rmsnorm(x, w):
    xf = x.astype(jnp.float32)
    y = xf * lax.rsqrt(jnp.mean(xf * xf, axis=-1, keepdims=True) + RMS_EPS)
    return (y * w.astype(jnp.float32)).astype(x.dtype)


def l2norm(x):
    return x * lax.rsqrt(jnp.sum(x * x, axis=-1, keepdims=True) + 1e-6)


def softcap(x):
    return GATE_SOFTCAP * jnp.tanh(x / GATE_SOFTCAP)


def split_cols(x, sizes):
    out, off = [], 0
    for s in sizes:
        out.append(x[..., off:off + s])
        off += s
    return out


def causal_conv(x, buf, w):
    T = x.shape[1]
    xp = jnp.concatenate([buf.astype(x.dtype), x], axis=1)
    y = sum(xp[:, j:j + T] * w[j].astype(x.dtype) for j in range(CONV_W))
    return y, xp[:, T:]


def to_chunks(a, c):
    b, t = a.shape[:2]
    a = a.reshape((b, t // c, c) + a.shape[2:])
    return jnp.swapaxes(jnp.swapaxes(a, 2, 3), 0, 1)


def from_chunks(a):
    n, b, h, c, d = a.shape
    return jnp.swapaxes(jnp.swapaxes(a, 0, 1), 2, 3).reshape(b, n * c, h, d)


def gdn_chunked(q, k, v, g, beta, S0):
    T = v.shape[1]
    c = math.gcd(T, GDN_CHUNK)
    causal = jnp.tril(jnp.ones((c, c), dtype=bool))
    strict = jnp.tril(jnp.ones((c, c), dtype=bool), -1)
    qc, kc, vc = to_chunks(q, c), to_chunks(k, c), to_chunks(v, c)
    gc = jnp.cumsum(to_chunks(g, c), axis=-1)
    bc = to_chunks(beta, c)
    diff = gc[..., :, None] - gc[..., None, :]
    decay = jnp.where(causal, jnp.exp(jnp.where(causal, diff, 0.0)), 0.0)
    kb = kc * bc[..., None]
    a_mat = jnp.where(strict, jnp.einsum('nbhid,nbhjd->nbhij', kb, kc) * decay, 0.0)
    rhs = jnp.concatenate([vc * bc[..., None], kb * jnp.exp(gc)[..., None]], axis=-1)
    sol = lax.linalg.triangular_solve(a_mat, rhs, left_side=True, lower=True, unit_diagonal=True)
    u, w = sol[..., :GDN_DV], sol[..., GDN_DV:]
    qk = jnp.where(causal, jnp.einsum('nbhid,nbhjd->nbhij', qc, kc) * decay, 0.0)

    def step(S, xs):
        qi, ki, ui, wi, gi, qki = xs
        v_new = ui - jnp.einsum('bhcd,bhde->bhce', wi, S)
        o = (jnp.einsum('bhcd,bhde->bhce', qi * jnp.exp(gi)[..., None], S)
             + jnp.einsum('bhij,bhje->bhie', qki, v_new))
        g_last = gi[..., -1]
        k_dec = ki * jnp.exp(g_last[..., None] - gi)[..., None]
        S = S * jnp.exp(g_last)[..., None, None] + jnp.einsum('bhcd,bhce->bhde', k_dec, v_new)
        return S, o

    S, o = lax.scan(step, S0, (qc, kc, u, w, gc, qk))
    return from_chunks(o), (S,)


def mlstm_chunked(q, k, v, i_pre, f_pre, C0, n0, m0):
    T = v.shape[1]
    c = math.gcd(T, ML_CHUNK)
    causal = jnp.tril(jnp.ones((c, c), dtype=bool))
    qc, kc, vc = to_chunks(q, c), to_chunks(k, c), to_chunks(v, c)
    li = to_chunks(i_pre, c)
    bcum = jnp.cumsum(to_chunks(jax.nn.log_sigmoid(f_pre), c), axis=-1)
    dlog = jnp.where(causal, bcum[..., :, None] - bcum[..., None, :] + li[..., None, :], -jnp.inf)
    dmax = jnp.max(dlog, axis=-1)
    qk = jnp.einsum('nbhid,nbhjd->nbhij', qc, kc)
    w_end = bcum[..., -1:] - bcum + li

    def step(carry, xs):
        Cs, ns, ms = carry
        qi, ki, vi, bi, di, dmi, qki, wi = xs
        inter = bi + ms[..., None]
        mt = jnp.maximum(inter, dmi)
        pw = jnp.exp(di - mt[..., None]) * qki
        sc_in = jnp.exp(inter - mt)
        num = sc_in[..., None] * jnp.einsum('bhcd,bhde->bhce', qi, Cs) + jnp.einsum('bhij,bhje->bhie', pw, vi)
        den = sc_in * jnp.einsum('bhcd,bhd->bhc', qi, ns) + jnp.sum(pw, axis=-1)
        h = num / jnp.maximum(jnp.abs(den), jnp.exp(-mt))[..., None]
        b_last = bi[..., -1]
        m_new = jnp.maximum(b_last + ms, jnp.max(wi, axis=-1))
        sc = jnp.exp(b_last + ms - m_new)
        kw = ki * jnp.exp(wi - m_new[..., None])[..., None]
        C_new = sc[..., None, None] * Cs + jnp.einsum('bhcd,bhce->bhde', kw, vi)
        n_new = sc[..., None] * ns + jnp.sum(kw, axis=-2)
        return (C_new, n_new, m_new), h

    (C, n, m), h = lax.scan(step, (C0, n0, m0), (qc, kc, vc, bcum, dlog, dmax, qk, w_end))
    return from_chunks(h), (C, n, m)


def run_with_lead(fn, seqs, state, lead):
    if lead == 0:
        return fn(*seqs, *state)
    out1, state = fn(*[s[:, :lead] for s in seqs], *state)
    out2, state = fn(*[s[:, lead:] for s in seqs], *state)
    return jnp.concatenate([out1, out2], axis=1), state


def moe_ffn(h, router_w, router_b, w_gate, b_gate, w_up, b_up, w_down, b_down):
    lead_shape = h.shape[:-1]
    x = h.reshape(-1, D_MODEL)
    n_tok = x.shape[0]
    n_assign = n_tok * TOP_K
    logits = jnp.dot(x, router_w).astype(jnp.float32) + router_b.astype(jnp.float32)
    top_val, top_idx = lax.top_k(logits, TOP_K)
    gates = jax.nn.softmax(top_val, axis=-1).reshape(n_assign)
    flat_e = top_idx.reshape(n_assign)
    order = jnp.argsort(flat_e)
    e_sorted = flat_e[order]
    tok_sorted = order // TOP_K
    counts = jnp.bincount(flat_e, length=N_EXPERTS)
    padded = (counts + MOE_BLOCK - 1) // MOE_BLOCK * MOE_BLOCK
    start = jnp.cumsum(counts) - counts
    pad_end = jnp.cumsum(padded)
    pad_start = pad_end - padded
    dest = pad_start[e_sorted] + jnp.arange(n_assign) - start[e_sorted]
    n_blocks = -(-n_assign // MOE_BLOCK) + N_EXPERTS
    buf = jnp.zeros((n_blocks * MOE_BLOCK, D_MODEL), x.dtype).at[dest].set(x[tok_sorted])
    block_e = jnp.minimum(jnp.searchsorted(pad_end, jnp.arange(n_blocks) * MOE_BLOCK, side='right'), N_EXPERTS - 1)

    def expert_block(args):
        xb, e = args
        gl = jnp.minimum(xb @ w_gate[e] + b_gate[e], SWIGLU_LIMIT)
        lin = jnp.clip(xb @ w_up[e] + b_up[e], -SWIGLU_LIMIT, SWIGLU_LIMIT)
        act = gl * jax.nn.sigmoid(SWIGLU_ALPHA * gl) * (lin + 1.0)
        return act @ w_down[e] + b_down[e]

    yb = lax.map(expert_block, (buf.reshape(n_blocks, MOE_BLOCK, D_MODEL), block_e))
    y = yb.reshape(-1, D_MODEL)[dest] * gates[order][:, None].astype(x.dtype)
    out = jnp.zeros((n_tok, D_MODEL), x.dtype).at[tok_sorted].add(y)
    return out.reshape(lead_shape + (D_MODEL,))


def hybrid_layer(x, state, p, lead):
    conv_buf, S0, C0, n0, m0 = state
    (nm_w, w_in, conv_w, a_log, dt_bias, g_norm_w, ml_gb, ml_nw, w_out,
     nf_w, r_w, r_b, wg, bg, wu, bu, wd, bd) = p
    f32 = jnp.float32
    B, T, _ = x.shape
    h = rmsnorm(x, nm_w)
    proj = h @ w_in
    gq, gk, gv, gz, gb, ga, mq, mk, mv, mo, mi, mf = split_cols(proj, IN_SPLITS)

    qkv, conv_new = causal_conv(jnp.concatenate([gq, gk, gv], axis=-1), conv_buf, conv_w)
    qkv = jax.nn.silu(qkv.astype(f32))
    q, k, v = split_cols(qkv, (GDN_KEY_DIM, GDN_KEY_DIM, GDN_VAL_DIM))
    rep = GDN_HV // GDN_HK
    q = jnp.repeat(l2norm(q.reshape(B, T, GDN_HK, GDN_DK)), rep, axis=2) * (GDN_DK ** -0.5)
    k = jnp.repeat(l2norm(k.reshape(B, T, GDN_HK, GDN_DK)), rep, axis=2)
    v = v.reshape(B, T, GDN_HV, GDN_DV)
    beta = jax.nn.sigmoid(gb.astype(f32))
    g = -jnp.exp(a_log.astype(f32)) * jax.nn.softplus(ga.astype(f32) + dt_bias.astype(f32))
    o, (S_new,) = run_with_lead(gdn_chunked, (q, k, v, g, beta), (S0.astype(f32),), lead)
    o = rmsnorm(o, g_norm_w) * jax.nn.silu(gz.reshape(B, T, GDN_HV, GDN_DV).astype(f32))
    gdn_out = o.reshape(B, T, GDN_VAL_DIM).astype(x.dtype)

    q = mq.reshape(B, T, ML_H, ML_DQK).astype(f32) * (ML_DQK ** -0.5)
    k = mk.reshape(B, T, ML_H, ML_DQK).astype(f32)
    v = mv.reshape(B, T, ML_H, ML_DV).astype(f32)
    i_pre = softcap(mi.astype(f32) + ml_gb[0].astype(f32))
    f_pre = softcap(mf.astype(f32) + ml_gb[1].astype(f32))
    hm, (C_new, n_new, m_new) = run_with_lead(
        mlstm_chunked, (q, k, v, i_pre, f_pre),
        (C0.astype(f32), n0.astype(f32), m0.astype(f32)), lead)
    hm = rmsnorm(hm, ml_nw.reshape(ML_H, ML_DV)) * jax.nn.sigmoid(mo.reshape(B, T, ML_H, ML_DV).astype(f32))
    ml_out = hm.reshape(B, T, ML_V_DIM).astype(x.dtype)

    x = x + jnp.concatenate([gdn_out, ml_out], axis=-1) @ w_out
    x = x + moe_ffn(rmsnorm(x, nf_w), r_w, r_b, wg, bg, wu, bu, wd, bd)
    return x, (conv_new, S_new, C_new, n_new, m_new)


def setup_inputs(seed: int = 0) -> dict:
    key = jax.random.key(seed)
    ks = iter(jax.random.split(key, 40))
    f32 = jnp.float32

    def nrm(shape, scale):
        return jax.random.normal(next(ks), shape, f32) * scale

    dt = jnp.exp(jax.random.uniform(next(ks), (DEPTH, GDN_HV), f32, math.log(1e-3), math.log(1e-1)))
    gdn_dt_bias = dt + jnp.log(-jnp.expm1(-dt))
    gdn_a_log = jnp.log(jax.random.uniform(next(ks), (DEPTH, GDN_HV), f32, 1.0, 16.0))
    ml_gate_bias = nrm((DEPTH, 2, ML_H), 0.1) + jnp.array([0.0, FGATE_BIAS_INIT], f32)[None, :, None]
    return {
        'x_prompt': nrm((BATCH, SEQ, D_MODEL), 1.0),
        'x_sample': nrm((DEC_BATCH, DEC_SEQ, D_MODEL), 1.0),
        'state_gdn_conv': nrm((DEPTH, DEC_BATCH, CONV_W - 1, GDN_CONV_DIM), 1.0),
        'state_gdn': nrm((DEPTH, DEC_BATCH, GDN_HV, GDN_DK, GDN_DV), 0.1),
        'state_mlstm_c': nrm((DEPTH, DEC_BATCH, ML_H, ML_DQK, ML_DV), 0.1),
        'state_mlstm_n': nrm((DEPTH, DEC_BATCH, ML_H, ML_DQK), 0.1),
        'state_mlstm_m': nrm((DEPTH, DEC_BATCH, ML_H), 1.0),
        'meta_tokens': nrm((N_META, D_MODEL), 1.0),
        'norm_mix_w': 1.0 + nrm((DEPTH, D_MODEL), 0.02),
        'w_in': nrm((DEPTH, D_MODEL, IN_DIM), D_MODEL ** -0.5),
        'gdn_conv_w': nrm((DEPTH, CONV_W, GDN_CONV_DIM), CONV_W ** -0.5),
        'gdn_a_log': gdn_a_log,
        'gdn_dt_bias': gdn_dt_bias,
        'gdn_norm_w': 1.0 + nrm((DEPTH, GDN_DV), 0.02),
        'ml_gate_bias': ml_gate_bias,
        'ml_norm_w': 1.0 + nrm((DEPTH, ML_V_DIM), 0.02),
        'w_out': nrm((DEPTH, MIX_DIM, D_MODEL), MIX_DIM ** -0.5),
        'norm_ffn_w': 1.0 + nrm((DEPTH, D_MODEL), 0.02),
        'router_w': nrm((DEPTH, D_MODEL, N_EXPERTS), D_MODEL ** -0.5),
        'router_b': nrm((DEPTH, N_EXPERTS), 0.01),
        'w_gate': nrm((DEPTH, N_EXPERTS, D_MODEL, D_FF), D_MODEL ** -0.5),
        'b_gate': nrm((DEPTH, N_EXPERTS, D_FF), 0.02),
        'w_up': nrm((DEPTH, N_EXPERTS, D_MODEL, D_FF), D_MODEL ** -0.5),
        'b_up': nrm((DEPTH, N_EXPERTS, D_FF), 0.02),
        'w_down': nrm((DEPTH, N_EXPERTS, D_FF, D_MODEL), D_FF ** -0.5),
        'b_down': nrm((DEPTH, N_EXPERTS, D_MODEL), 0.02),
        'norm_final_w': 1.0 + nrm((D_MODEL,), 0.02),
    }


def reference(x_prompt, x_sample, state_gdn_conv, state_gdn, state_mlstm_c, state_mlstm_n, state_mlstm_m,
              meta_tokens, norm_mix_w, w_in, gdn_conv_w, gdn_a_log, gdn_dt_bias, gdn_norm_w,
              ml_gate_bias, ml_norm_w, w_out, norm_ffn_w, router_w, router_b,
              w_gate, b_gate, w_up, b_up, w_down, b_down, norm_final_w):
    f32 = jnp.float32
    B = x_prompt.shape[0]
    meta = jnp.broadcast_to(meta_tokens.astype(x_prompt.dtype), (B, N_META, D_MODEL))
    xp = jnp.concatenate([meta, x_prompt], axis=1)
    xs = x_sample
    p_new, s_new = [], []
    for l in range(DEPTH):
        params = (norm_mix_w[l], w_in[l], gdn_conv_w[l], gdn_a_log[l], gdn_dt_bias[l], gdn_norm_w[l],
                  ml_gate_bias[l], ml_norm_w[l], w_out[l], norm_ffn_w[l], router_w[l], router_b[l],
                  w_gate[l], b_gate[l], w_up[l], b_up[l], w_down[l], b_down[l])
        p_init = (jnp.zeros((B, CONV_W - 1, GDN_CONV_DIM), xp.dtype),
                  jnp.zeros((B, GDN_HV, GDN_DK, GDN_DV), f32),
                  jnp.zeros((B, ML_H, ML_DQK, ML_DV), f32),
                  jnp.zeros((B, ML_H, ML_DQK), f32),
                  jnp.zeros((B, ML_H), f32))
        xp, p_st = hybrid_layer(xp, p_init, params, N_META)
        s_init = (state_gdn_conv[l], state_gdn[l], state_mlstm_c[l], state_mlstm_n[l], state_mlstm_m[l])
        xs, s_st = hybrid_layer(xs, s_init, params, 0)
        p_new.append(p_st)
        s_new.append(s_st)
    y_prompt = rmsnorm(xp[:, N_META:], norm_final_w)
    y_sample = rmsnorm(xs, norm_final_w)

    def stack(states, i, like):
        return jnp.stack([st[i] for st in states]).astype(like.dtype)

    p_conv = stack(p_new, 0, state_gdn_conv)
    p_gdn = stack(p_new, 1, state_gdn)
    p_c = stack(p_new, 2, state_mlstm_c)
    p_n = stack(p_new, 3, state_mlstm_n)
    p_m = stack(p_new, 4, state_mlstm_m)
    s_conv = stack(s_new, 0, state_gdn_conv)
    s_gdn = stack(s_new, 1, state_gdn)
    s_c = stack(s_new, 2, state_mlstm_c)
    s_n = stack(s_new, 3, state_mlstm_n)
    s_m = stack(s_new, 4, state_mlstm_m)
    return (y_prompt, y_sample, p_conv, p_gdn, p_c, p_n, p_m, s_conv, s_gdn, s_c, s_n, s_m)
```

```python
import functools
import math

import jax
import jax.numpy as jnp
from jax import lax
from jax.experimental import pallas as pl
from jax.experimental.pallas import tpu as pltpu

F32 = jnp.float32
BF16 = jnp.bfloat16

D_MODEL = 4096
N_META = 16
RMS_EPS = 1e-6
GDN_DK = 128
GDN_DV = 128
GDN_HV = 16
GDN_HK = 8
GDN_KEY_DIM = GDN_HK * GDN_DK
GDN_VAL_DIM = GDN_HV * GDN_DV
GDN_CONV_DIM = 2 * GDN_KEY_DIM + GDN_VAL_DIM
CONV_W = 4
GDN_CHUNK = 64
ML_DQK = 256
ML_DV = 512
ML_H = 4
ML_QK_DIM = ML_H * ML_DQK
ML_V_DIM = ML_H * ML_DV
ML_CHUNK = 64
GATE_SOFTCAP = 15.0
IN_SPLITS = (GDN_KEY_DIM, GDN_KEY_DIM, GDN_VAL_DIM, GDN_VAL_DIM, GDN_HV, GDN_HV,
             ML_QK_DIM, ML_QK_DIM, ML_V_DIM, ML_V_DIM, ML_H, ML_H)
N_EXPERTS = 32
TOP_K = 4
D_FF = D_MODEL
SWIGLU_LIMIT = 7.0
SWIGLU_ALPHA = 1.702

LANES = 128
VMEM_LIMIT = 56 * 1024 * 1024
MOE_TM = 256
NEG_BIG = -1e30


def _rmsnorm_kernel(x_ref, w_ref, o_ref):
    x = x_ref[...]
    y = x * lax.rsqrt(jnp.mean(x * x, axis=-1, keepdims=True) + RMS_EPS)
    o_ref[...] = (y * w_ref[...]).astype(o_ref.dtype)


def _rmsnorm(x, w, out_dtype, tr):
    n, d = x.shape
    return pl.pallas_call(
        _rmsnorm_kernel,
        grid=(n // tr,),
        in_specs=[pl.BlockSpec((tr, d), lambda i: (i, 0)),
                  pl.BlockSpec((1, d), lambda i: (0, 0))],
        out_specs=pl.BlockSpec((tr, d), lambda i: (i, 0)),
        out_shape=jax.ShapeDtypeStruct((n, d), out_dtype),
        compiler_params=pltpu.CompilerParams(
            dimension_semantics=("arbitrary",), vmem_limit_bytes=VMEM_LIMIT),
        name="rmsnorm",
    )(x, w.reshape(1, d))


def _mm_kernel(x_ref, w_ref, o_ref, wb_ref):
    @pl.when(pl.program_id(1) == 0)
    def _():
        wb_ref[...] = w_ref[...].astype(BF16)

    o_ref[...] = jnp.dot(x_ref[...], wb_ref[...], preferred_element_type=F32)


def _mm_res_kernel(x_ref, w_ref, r_ref, o_ref, wb_ref):
    @pl.when(pl.program_id(1) == 0)
    def _():
        wb_ref[...] = w_ref[...].astype(BF16)

    o_ref[...] = r_ref[...] + jnp.dot(x_ref[...], wb_ref[...], preferred_element_type=F32)


def _matmul(x, w, tm, tn, residual=None):
    m, k = x.shape
    n = w.shape[1]
    in_specs = [pl.BlockSpec((tm, k), lambda j, i: (i, 0)),
                pl.BlockSpec((k, tn), lambda j, i: (0, j))]
    args = [x, w]
    body = _mm_kernel
    if residual is not None:
        in_specs.append(pl.BlockSpec((tm, tn), lambda j, i: (i, j)))
        args.append(residual)
        body = _mm_res_kernel
    return pl.pallas_call(
        body,
        grid=(n // tn, m // tm),
        in_specs=in_specs,
        out_specs=pl.BlockSpec((tm, tn), lambda j, i: (i, j)),
        out_shape=jax.ShapeDtypeStruct((m, n), F32),
        scratch_shapes=[pltpu.VMEM((k, tn), BF16)],
        compiler_params=pltpu.CompilerParams(
            dimension_semantics=("arbitrary", "arbitrary"), vmem_limit_bytes=VMEM_LIMIT),
        name="matmul",
    )(*args)


def _router_kernel(x_ref, nw_ref, rw_ref, rb_ref, h_ref, gate_ref, idx_ref):
    x = x_ref[...]
    h = x * lax.rsqrt(jnp.mean(x * x, axis=-1, keepdims=True) + RMS_EPS) * nw_ref[...]
    h_ref[...] = h.astype(h_ref.dtype)
    logits = jnp.dot(h, rw_ref[...], preferred_element_type=F32,
                     precision=lax.Precision.HIGHEST) + rb_ref[...]
    lane = lax.broadcasted_iota(jnp.int32, logits.shape, 1)
    vals, idxs = [], []
    cur = logits
    for _ in range(TOP_K):
        mx = jnp.max(cur, axis=-1, keepdims=True)
        ix = jnp.min(jnp.where(cur == mx, lane, LANES), axis=-1, keepdims=True)
        vals.append(mx)
        idxs.append(ix)
        cur = jnp.where(lane == ix, NEG_BIG, cur)
    exps = [jnp.exp(v - vals[0]) for v in vals]
    denom = exps[0] + exps[1] + exps[2] + exps[3]
    gate_out = jnp.zeros(logits.shape, F32)
    idx_out = jnp.zeros(logits.shape, jnp.int32)
    for k in range(TOP_K):
        gate_out = jnp.where(lane == k, exps[k] / denom, gate_out)
        idx_out = jnp.where(lane == k, idxs[k], idx_out)
    gate_ref[...] = gate_out
    idx_ref[...] = idx_out


def _router(x, norm_w, router_w, router_b, tr):
    n, d = x.shape
    rw = jnp.zeros((d, LANES), F32).at[:, :N_EXPERTS].set(router_w)
    rb = jnp.full((1, LANES), NEG_BIG, F32).at[0, :N_EXPERTS].set(router_b)
    h, gates, idx = pl.pallas_call(
        _router_kernel,
        grid=(n // tr,),
        in_specs=[pl.BlockSpec((tr, d), lambda i: (i, 0)),
                  pl.BlockSpec((1, d), lambda i: (0, 0)),
                  pl.BlockSpec((d, LANES), lambda i: (0, 0)),
                  pl.BlockSpec((1, LANES), lambda i: (0, 0))],
        out_specs=[pl.BlockSpec((tr, d), lambda i: (i, 0)),
                   pl.BlockSpec((tr, LANES), lambda i: (i, 0)),
                   pl.BlockSpec((tr, LANES), lambda i: (i, 0))],
        out_shape=[jax.ShapeDtypeStruct((n, d), BF16),
                   jax.ShapeDtypeStruct((n, LANES), F32),
                   jax.ShapeDtypeStruct((n, LANES), jnp.int32)],
        compiler_params=pltpu.CompilerParams(
            dimension_semantics=("arbitrary",), vmem_limit_bytes=VMEM_LIMIT),
        name="router",
    )(x, norm_w.reshape(1, d), rw, rb)
    return h, gates[:, :TOP_K], idx[:, :TOP_K]


def _expert_changed(be_ref, b):
    prev = be_ref[jnp.maximum(b - 1, 0)]
    return jnp.logical_or(b == 0, be_ref[b] != prev)


def _moe_up_kernel(be_ref, nv_ref, x_ref, wg_ref, wu_ref, bg_ref, bu_ref, h_ref, wgb_ref, wub_ref):
    b = pl.program_id(1)

    @pl.when(_expert_changed(be_ref, b))
    def _():
        wgb_ref[...] = wg_ref[...].astype(BF16)
        wub_ref[...] = wu_ref[...].astype(BF16)

    @pl.when(b < nv_ref[0])
    def _():
        x = x_ref[...]
        g = jnp.dot(x, wgb_ref[...], preferred_element_type=F32) + bg_ref[...]
        u = jnp.dot(x, wub_ref[...], preferred_element_type=F32) + bu_ref[...]
        gl = jnp.minimum(g, SWIGLU_LIMIT)
        lin = jnp.clip(u, -SWIGLU_LIMIT, SWIGLU_LIMIT)
        act = gl * jax.nn.sigmoid(SWIGLU_ALPHA * gl) * (lin + 1.0)
        h_ref[...] = act.astype(h_ref.dtype)

    @pl.when(b >= nv_ref[0])
    def _():
        h_ref[...] = jnp.zeros(h_ref.shape, h_ref.dtype)


def _moe_down_kernel(be_ref, nv_ref, h_ref, wd_ref, bd_ref, gate_ref, y_ref, wdb_ref):
    b = pl.program_id(1)

    @pl.when(_expert_changed(be_ref, b))
    def _():
        wdb_ref[...] = wd_ref[...].astype(BF16)

    @pl.when(b < nv_ref[0])
    def _():
        y = jnp.dot(h_ref[...], wdb_ref[...], preferred_element_type=F32) + bd_ref[...]
        y_ref[...] = y * gate_ref[...]

    @pl.when(b >= nv_ref[0])
    def _():
        y_ref[...] = jnp.zeros(y_ref.shape, y_ref.dtype)


def _moe_experts(xs, gate_rows, block_e, n_valid, w_gate, b_gate, w_up, b_up, w_down, b_down, tn):
    r, d = xs.shape
    nb = r // MOE_TM
    dff = w_gate.shape[-1]
    cp = pltpu.CompilerParams(dimension_semantics=("arbitrary", "arbitrary"),
                              vmem_limit_bytes=VMEM_LIMIT)
    h = pl.pallas_call(
        _moe_up_kernel,
        grid_spec=pltpu.PrefetchScalarGridSpec(
            num_scalar_prefetch=2,
            grid=(dff // tn, nb),
            in_specs=[pl.BlockSpec((MOE_TM, d), lambda j, b, be, nv: (b, 0)),
                      pl.BlockSpec((None, d, tn), lambda j, b, be, nv: (be[b], 0, j)),
                      pl.BlockSpec((None, d, tn), lambda j, b, be, nv: (be[b], 0, j)),
                      pl.BlockSpec((None, 1, tn), lambda j, b, be, nv: (be[b], 0, j)),
                      pl.BlockSpec((None, 1, tn), lambda j, b, be, nv: (be[b], 0, j))],
            out_specs=pl.BlockSpec((MOE_TM, tn), lambda j, b, be, nv: (b, j)),
            scratch_shapes=[pltpu.VMEM((d, tn), BF16), pltpu.VMEM((d, tn), BF16)]),
        out_shape=jax.ShapeDtypeStruct((r, dff), BF16),
        compiler_params=cp,
        name="moe_up",
    )(block_e, n_valid, xs, w_gate, w_up,
      b_gate.reshape(N_EXPERTS, 1, dff), b_up.reshape(N_EXPERTS, 1, dff))
    y = pl.pallas_call(
        _moe_down_kernel,
        grid_spec=pltpu.PrefetchScalarGridSpec(
            num_scalar_prefetch=2,
            grid=(d // tn, nb),
            in_specs=[pl.BlockSpec((MOE_TM, dff), lambda j, b, be, nv: (b, 0)),
                      pl.BlockSpec((None, dff, tn), lambda j, b, be, nv: (be[b], 0, j)),
                      pl.BlockSpec((None, 1, tn), lambda j, b, be, nv: (be[b], 0, j)),
                      pl.BlockSpec((MOE_TM, 1), lambda j, b, be, nv: (b, 0))],
            out_specs=pl.BlockSpec((MOE_TM, tn), lambda j, b, be, nv: (b, j)),
            scratch_shapes=[pltpu.VMEM((dff, tn), BF16)]),
        out_shape=jax.ShapeDtypeStruct((r, d), F32),
        compiler_params=cp,
        name="moe_down",
    )(block_e, n_valid, h, w_down, b_down.reshape(N_EXPERTS, 1, d), gate_rows)
    return y


def _l2norm(x):
    return x * lax.rsqrt(jnp.sum(x * x, axis=-1, keepdims=True) + 1e-6)


def _softcap(x):
    return GATE_SOFTCAP * jnp.tanh(x / GATE_SOFTCAP)


def _to_chunks(a, c):
    b, t = a.shape[:2]
    a = a.reshape((b, t // c, c) + a.shape[2:])
    return jnp.swapaxes(jnp.swapaxes(a, 2, 3), 0, 1)


def _from_chunks(a):
    n, b, h, c, d = a.shape
    return jnp.swapaxes(jnp.swapaxes(a, 0, 1), 2, 3).reshape(b, n * c, h, d)


def _gdn_chunked(q, k, v, g, beta, S0):
    T = v.shape[1]
    c = math.gcd(T, GDN_CHUNK)
    causal = jnp.tril(jnp.ones((c, c), dtype=bool))
    strict = jnp.tril(jnp.ones((c, c), dtype=bool), -1)
    qc, kc, vc = _to_chunks(q, c), _to_chunks(k, c), _to_chunks(v, c)
    gc = jnp.cumsum(_to_chunks(g, c), axis=-1)
    bc = _to_chunks(beta, c)
    diff = gc[..., :, None] - gc[..., None, :]
    decay = jnp.where(causal, jnp.exp(jnp.where(causal, diff, 0.0)), 0.0)
    kb = kc * bc[..., None]
    a_mat = jnp.where(strict, jnp.einsum('nbhid,nbhjd->nbhij', kb, kc) * decay, 0.0)
    rhs = jnp.concatenate([vc * bc[..., None], kb * jnp.exp(gc)[..., None]], axis=-1)
    sol = lax.linalg.triangular_solve(a_mat, rhs, left_side=True, lower=True, unit_diagonal=True)
    u, w = sol[..., :GDN_DV], sol[..., GDN_DV:]
    qk = jnp.where(causal, jnp.einsum('nbhid,nbhjd->nbhij', qc, kc) * decay, 0.0)

    def step(S, xs):
        qi, ki, ui, wi, gi, qki = xs
        v_new = ui - jnp.einsum('bhcd,bhde->bhce', wi, S)
        o = (jnp.einsum('bhcd,bhde->bhce', qi * jnp.exp(gi)[..., None], S)
             + jnp.einsum('bhij,bhje->bhie', qki, v_new))
        g_last = gi[..., -1]
        k_dec = ki * jnp.exp(g_last[..., None] - gi)[..., None]
        S = S * jnp.exp(g_last)[..., None, None] + jnp.einsum('bhcd,bhce->bhde', k_dec, v_new)
        return S, o

    S, o = lax.scan(step, S0, (qc, kc, u, w, gc, qk))
    return _from_chunks(o), (S,)


def _mlstm_chunked(q, k, v, i_pre, f_pre, C0, n0, m0):
    T = v.shape[1]
    c = math.gcd(T, ML_CHUNK)
    causal = jnp.tril(jnp.ones((c, c), dtype=bool))
    qc, kc, vc = _to_chunks(q, c), _to_chunks(k, c), _to_chunks(v, c)
    li = _to_chunks(i_pre, c)
    bcum = jnp.cumsum(_to_chunks(jax.nn.log_sigmoid(f_pre), c), axis=-1)
    dlog = jnp.where(causal, bcum[..., :, None] - bcum[..., None, :] + li[..., None, :], -jnp.inf)
    dmax = jnp.max(dlog, axis=-1)
    qk = jnp.einsum('nbhid,nbhjd->nbhij', qc, kc)
    w_end = bcum[..., -1:] - bcum + li

    def step(carry, xs):
        Cs, ns, ms = carry
        qi, ki, vi, bi, di, dmi, qki, wi = xs
        inter = bi + ms[..., None]
        mt = jnp.maximum(inter, dmi)
        pw = jnp.exp(di - mt[..., None]) * qki
        sc_in = jnp.exp(inter - mt)
        num = sc_in[..., None] * jnp.einsum('bhcd,bhde->bhce', qi, Cs) + jnp.einsum('bhij,bhje->bhie', pw, vi)
        den = sc_in * jnp.einsum('bhcd,bhd->bhc', qi, ns) + jnp.sum(pw, axis=-1)
        h = num / jnp.maximum(jnp.abs(den), jnp.exp(-mt))[..., None]
        b_last = bi[..., -1]
        m_new = jnp.maximum(b_last + ms, jnp.max(wi, axis=-1))
        sc = jnp.exp(b_last + ms - m_new)
        kw = ki * jnp.exp(wi - m_new[..., None])[..., None]
        C_new = sc[..., None, None] * Cs + jnp.einsum('bhcd,bhce->bhde', kw, vi)
        n_new = sc[..., None] * ns + jnp.sum(kw, axis=-2)
        return (C_new, n_new, m_new), h

    (C, n, m), h = lax.scan(step, (C0, n0, m0), (qc, kc, vc, bcum, dlog, dmax, qk, w_end))
    return _from_chunks(h), (C, n, m)


def _run_with_lead(fn, seqs, state, lead):
    if lead == 0:
        return fn(*seqs, *state)
    out1, state = fn(*[s[:, :lead] for s in seqs], *state)
    out2, state = fn(*[s[:, lead:] for s in seqs], *state)
    return jnp.concatenate([out1, out2], axis=1), state


def _rms(x, w):
    y = x * lax.rsqrt(jnp.mean(x * x, axis=-1, keepdims=True) + RMS_EPS)
    return y * w


def _mixers(pm, ps, state, p, lead):
    conv_buf, S0, C0, n0, m0 = state
    conv_w, a_log, dt_bias, g_norm_w, ml_gb, ml_nw = p
    B, T, _ = pm.shape
    qkv_pre = pm[..., :GDN_CONV_DIM]
    gz = pm[..., GDN_CONV_DIM:GDN_CONV_DIM + GDN_VAL_DIM]
    o = GDN_CONV_DIM + GDN_VAL_DIM
    mq = pm[..., o:o + ML_QK_DIM]
    mk = pm[..., o + ML_QK_DIM:o + 2 * ML_QK_DIM]
    mv = pm[..., o + 2 * ML_QK_DIM:o + 2 * ML_QK_DIM + ML_V_DIM]
    mo = pm[..., o + 2 * ML_QK_DIM + ML_V_DIM:]
    gb = ps[..., :GDN_HV]
    ga = ps[..., GDN_HV:2 * GDN_HV]
    mi = ps[..., 2 * GDN_HV:2 * GDN_HV + ML_H]
    mf = ps[..., 2 * GDN_HV + ML_H:2 * GDN_HV + 2 * ML_H]

    xp = jnp.concatenate([conv_buf, qkv_pre], axis=1)
    qkv = sum(xp[:, j:j + T] * conv_w[j] for j in range(CONV_W))
    conv_new = xp[:, T:]
    qkv = jax.nn.silu(qkv)
    q = qkv[..., :GDN_KEY_DIM]
    k = qkv[..., GDN_KEY_DIM:2 * GDN_KEY_DIM]
    v = qkv[..., 2 * GDN_KEY_DIM:]
    rep = GDN_HV // GDN_HK
    q = jnp.repeat(_l2norm(q.reshape(B, T, GDN_HK, GDN_DK)), rep, axis=2) * (GDN_DK ** -0.5)
    k = jnp.repeat(_l2norm(k.reshape(B, T, GDN_HK, GDN_DK)), rep, axis=2)
    v = v.reshape(B, T, GDN_HV, GDN_DV)
    beta = jax.nn.sigmoid(gb)
    g = -jnp.exp(a_log) * jax.nn.softplus(ga + dt_bias)
    og, (S_new,) = _run_with_lead(_gdn_chunked, (q, k, v, g, beta), (S0,), lead)
    og = _rms(og, g_norm_w) * jax.nn.silu(gz.reshape(B, T, GDN_HV, GDN_DV))
    gdn_out = og.reshape(B, T, GDN_VAL_DIM)

    q = mq.reshape(B, T, ML_H, ML_DQK) * (ML_DQK ** -0.5)
    k = mk.reshape(B, T, ML_H, ML_DQK)
    v = mv.reshape(B, T, ML_H, ML_DV)
    i_pre = _softcap(mi + ml_gb[0])
    f_pre = _softcap(mf + ml_gb[1])
    hm, (C_new, n_new, m_new) = _run_with_lead(
        _mlstm_chunked, (q, k, v, i_pre, f_pre), (C0, n0, m0), lead)
    hm = _rms(hm, ml_nw.reshape(ML_H, ML_DV)) * jax.nn.sigmoid(mo.reshape(B, T, ML_H, ML_DV))
    ml_out = hm.reshape(B, T, ML_V_DIM)
    return jnp.concatenate([gdn_out, ml_out], axis=-1), (conv_new, S_new, C_new, n_new, m_new)


def kernel(x_prompt, x_sample, state_gdn_conv, state_gdn, state_mlstm_c, state_mlstm_n, state_mlstm_m,
           meta_tokens, norm_mix_w, w_in, gdn_conv_w, gdn_a_log, gdn_dt_bias, gdn_norm_w,
           ml_gate_bias, ml_norm_w, w_out, norm_ffn_w, router_w, router_b,
           w_gate, b_gate, w_up, b_up, w_down, b_down, norm_final_w):
    B, S, D = x_prompt.shape
    BS, TS, _ = x_sample.shape
    TP = S + N_META
    n_p = B * TP
    n_tok = n_p + BS * TS
    meta = jnp.broadcast_to(meta_tokens.astype(x_prompt.dtype), (B, N_META, D))
    x_all = jnp.concatenate([jnp.concatenate([meta, x_prompt], axis=1).reshape(n_p, D),
                             x_sample.reshape(BS * TS, D)], axis=0)

    w = w_in[0]
    offs = [0]
    for s in IN_SPLITS:
        offs.append(offs[-1] + s)
    col = lambda i: w[:, offs[i]:offs[i + 1]]
    w_main = jnp.concatenate([col(0), col(1), col(2), col(3), col(6), col(7), col(8), col(9)], axis=1)
    w_small = jnp.concatenate([col(4), col(5), col(10), col(11)], axis=1)
    w_small = jnp.pad(w_small, ((0, 0), (0, LANES - w_small.shape[1])))

    tm = n_tok // 10
    h = _rmsnorm(x_all, norm_mix_w[0], BF16, n_tok // 20)
    proj_main = _matmul(h, w_main, tm, 512)
    proj_small = _matmul(h, w_small, tm, LANES)

    mix_p = (gdn_conv_w[0], gdn_a_log[0], gdn_dt_bias[0], gdn_norm_w[0], ml_gate_bias[0], ml_norm_w[0])
    p_init = (jnp.zeros((B, CONV_W - 1, GDN_CONV_DIM), F32),
              jnp.zeros((B, GDN_HV, GDN_DK, GDN_DV), F32),
              jnp.zeros((B, ML_H, ML_DQK, ML_DV), F32),
              jnp.zeros((B, ML_H, ML_DQK), F32),
              jnp.zeros((B, ML_H), F32))
    mixed_p, p_st = _mixers(proj_main[:n_p].reshape(B, TP, -1), proj_small[:n_p].reshape(B, TP, -1),
                            p_init, mix_p, N_META)
    s_init = (state_gdn_conv[0], state_gdn[0], state_mlstm_c[0], state_mlstm_n[0], state_mlstm_m[0])
    mixed_s, s_st = _mixers(proj_main[n_p:].reshape(BS, TS, -1), proj_small[n_p:].reshape(BS, TS, -1),
                            s_init, mix_p, 0)
    mixed = jnp.concatenate([mixed_p.reshape(n_p, D), mixed_s.reshape(BS * TS, D)], axis=0).astype(BF16)

    x1 = _matmul(mixed, w_out[0], tm, 512, residual=x_all)

    h2, gates, top_idx = _router(x1, norm_ffn_w[0], router_w[0], router_b[0], n_tok // 20)
    n_assign = n_tok * TOP_K
    flat_e = top_idx.reshape(n_assign)
    order = jnp.argsort(flat_e)
    e_sorted = flat_e[order]
    tok_sorted = order // TOP_K
    counts = jnp.bincount(flat_e, length=N_EXPERTS)
    padded = (counts + MOE_TM - 1) // MOE_TM * MOE_TM
    start = jnp.cumsum(counts) - counts
    pad_end = jnp.cumsum(padded)
    pad_start = pad_end - padded
    dest = pad_start[e_sorted] + jnp.arange(n_assign) - start[e_sorted]
    n_blocks = -(-n_assign // MOE_TM) + N_EXPERTS
    block_e = jnp.minimum(jnp.searchsorted(pad_end, jnp.arange(n_blocks) * MOE_TM, side='right'),
                          N_EXPERTS - 1).astype(jnp.int32)
    n_valid = (pad_end[-1] // MOE_TM).astype(jnp.int32).reshape(1)
    xs = jnp.zeros((n_blocks * MOE_TM, D), BF16).at[dest].set(h2[tok_sorted])
    gate_rows = jnp.zeros((n_blocks * MOE_TM, 1), F32).at[dest, 0].set(gates.reshape(n_assign)[order])
    y = _moe_experts(xs, gate_rows, block_e, n_valid, w_gate[0], b_gate[0], w_up[0], b_up[0],
                     w_down[0], b_down[0], 256)
    dest_by_assign = jnp.zeros((n_assign,), jnp.int32).at[order].set(dest.astype(jnp.int32))
    x2 = x1 + y[dest_by_assign].reshape(n_tok, TOP_K, D).sum(axis=1)

    y_all = _rmsnorm(x2, norm_final_w, F32, n_tok // 20)
    y_prompt = y_all[:n_p].reshape(B, TP, D)[:, N_META:]
    y_sample = y_all[n_p:].reshape(BS, TS, D)

    def lead1(a, like):
        return a[None].astype(like.dtype)

    likes = (state_gdn_conv, state_gdn, state_mlstm_c, state_mlstm_n, state_mlstm_m)
    p_out = tuple(lead1(a, l) for a, l in zip(p_st, likes))
    s_out = tuple(lead1(a, l) for a, l in zip(s_st, likes))
    return (y_prompt, y_sample) + p_out + s_out
```

```python
import functools
import math

import jax
import jax.numpy as jnp
from jax import lax
from jax.experimental import pallas as pl
from jax.experimental.pallas import tpu as pltpu

F32 = jnp.float32
BF16 = jnp.bfloat16
HP = lax.Precision.HIGHEST

D_MODEL = 4096
N_META = 16
RMS_EPS = 1e-6
GDN_DK = 128
GDN_DV = 128
GDN_HV = 16
GDN_HK = 8
GDN_KEY_DIM = GDN_HK * GDN_DK
GDN_VAL_DIM = GDN_HV * GDN_DV
GDN_CONV_DIM = 2 * GDN_KEY_DIM + GDN_VAL_DIM
CONV_W = 4
GDN_CHUNK = 64
ML_DQK = 256
ML_DV = 512
ML_H = 4
ML_QK_DIM = ML_H * ML_DQK
ML_V_DIM = ML_H * ML_DV
ML_CHUNK = 64
GATE_SOFTCAP = 15.0
IN_SPLITS = (GDN_KEY_DIM, GDN_KEY_DIM, GDN_VAL_DIM, GDN_VAL_DIM, GDN_HV, GDN_HV,
             ML_QK_DIM, ML_QK_DIM, ML_V_DIM, ML_V_DIM, ML_H, ML_H)
N_EXPERTS = 32
TOP_K = 4
SWIGLU_LIMIT = 7.0
SWIGLU_ALPHA = 1.702

LANES = 128
VMEM_LIMIT = 56 * 1024 * 1024
MOE_TM = 256
NEG_BIG = -1e30
HEAD_GROUP = 8
HALO = 8
GDN_COLS = GDN_CONV_DIM + GDN_VAL_DIM
ML_COLS = 2 * ML_QK_DIM + 2 * ML_V_DIM
LANE_DT = GDN_HV
LANE_I = 2 * GDN_HV
LANE_F = 2 * GDN_HV + ML_H


def _dot(a, b, precision=None):
    return jnp.dot(a, b, preferred_element_type=F32, precision=precision)


def _dot_nt(a, b):
    return lax.dot_general(a, b, (((1,), (1,)), ((), ())), preferred_element_type=F32)


def _dot_tn(a, b):
    return lax.dot_general(a, b, (((0,), (0,)), ((), ())), preferred_element_type=F32)


def _rmsnorm_kernel(x_ref, w_ref, o_ref):
    x = x_ref[...]
    y = x * lax.rsqrt(jnp.mean(x * x, axis=-1, keepdims=True) + RMS_EPS)
    o_ref[...] = (y * w_ref[...]).astype(o_ref.dtype)


def _rmsnorm(x, w, out_dtype, tr):
    n, d = x.shape
    return pl.pallas_call(
        _rmsnorm_kernel,
        grid=(n // tr,),
        in_specs=[pl.BlockSpec((tr, d), lambda i: (i, 0)),
                  pl.BlockSpec((1, d), lambda i: (0, 0))],
        out_specs=pl.BlockSpec((tr, d), lambda i: (i, 0)),
        out_shape=jax.ShapeDtypeStruct((n, d), out_dtype),
        compiler_params=pltpu.CompilerParams(
            dimension_semantics=("arbitrary",), vmem_limit_bytes=VMEM_LIMIT),
        name="rmsnorm",
    )(x, w.reshape(1, d))


def _mm_kernel(x_ref, w_ref, o_ref, wb_ref):
    @pl.when(pl.program_id(1) == 0)
    def _():
        wb_ref[...] = w_ref[...].astype(BF16)

    o_ref[...] = _dot(x_ref[...], wb_ref[...])


def _mm_res_kernel(x_ref, w_ref, r_ref, o_ref, wb_ref):
    @pl.when(pl.program_id(1) == 0)
    def _():
        wb_ref[...] = w_ref[...].astype(BF16)

    o_ref[...] = r_ref[...] + _dot(x_ref[...], wb_ref[...])


def _matmul(x, w, tm, tn, residual=None):
    m, k = x.shape
    n = w.shape[1]
    in_specs = [pl.BlockSpec((tm, k), lambda j, i: (i, 0)),
                pl.BlockSpec((k, tn), lambda j, i: (0, j))]
    args = [x, w]
    body = _mm_kernel
    if residual is not None:
        in_specs.append(pl.BlockSpec((tm, tn), lambda j, i: (i, j)))
        args.append(residual)
        body = _mm_res_kernel
    return pl.pallas_call(
        body,
        grid=(n // tn, m // tm),
        in_specs=in_specs,
        out_specs=pl.BlockSpec((tm, tn), lambda j, i: (i, j)),
        out_shape=jax.ShapeDtypeStruct((m, n), F32),
        scratch_shapes=[pltpu.VMEM((k, tn), BF16)],
        compiler_params=pltpu.CompilerParams(
            dimension_semantics=("arbitrary", "arbitrary"), vmem_limit_bytes=VMEM_LIMIT),
        name="matmul",
    )(*args)


def _router_kernel(x_ref, nw_ref, rw_ref, rb_ref, h_ref, gate_ref, idx_ref):
    x = x_ref[...]
    h = x * lax.rsqrt(jnp.mean(x * x, axis=-1, keepdims=True) + RMS_EPS) * nw_ref[...]
    h_ref[...] = h.astype(h_ref.dtype)
    logits = _dot(h, rw_ref[...], HP) + rb_ref[...]
    lane = lax.broadcasted_iota(jnp.int32, logits.shape, 1)
    vals, idxs = [], []
    cur = logits
    for _ in range(TOP_K):
        mx = jnp.max(cur, axis=-1, keepdims=True)
        ix = jnp.min(jnp.where(cur == mx, lane, LANES), axis=-1, keepdims=True)
        vals.append(mx)
        idxs.append(ix)
        cur = jnp.where(lane == ix, NEG_BIG, cur)
    exps = [jnp.exp(v - vals[0]) for v in vals]
    denom = exps[0] + exps[1] + exps[2] + exps[3]
    gate_out = jnp.zeros(logits.shape, F32)
    idx_out = jnp.zeros(logits.shape, jnp.int32)
    for k in range(TOP_K):
        gate_out = jnp.where(lane == k, exps[k] / denom, gate_out)
        idx_out = jnp.where(lane == k, idxs[k], idx_out)
    gate_ref[...] = gate_out
    idx_ref[...] = idx_out


def _router(x, norm_w, router_w, router_b, tr):
    n, d = x.shape
    rw = jnp.zeros((d, LANES), F32).at[:, :N_EXPERTS].set(router_w)
    rb = jnp.full((1, LANES), NEG_BIG, F32).at[0, :N_EXPERTS].set(router_b)
    h, gates, idx = pl.pallas_call(
        _router_kernel,
        grid=(n // tr,),
        in_specs=[pl.BlockSpec((tr, d), lambda i: (i, 0)),
                  pl.BlockSpec((1, d), lambda i: (0, 0)),
                  pl.BlockSpec((d, LANES), lambda i: (0, 0)),
                  pl.BlockSpec((1, LANES), lambda i: (0, 0))],
        out_specs=[pl.BlockSpec((tr, d), lambda i: (i, 0)),
                   pl.BlockSpec((tr, LANES), lambda i: (i, 0)),
                   pl.BlockSpec((tr, LANES), lambda i: (i, 0))],
        out_shape=[jax.ShapeDtypeStruct((n, d), BF16),
                   jax.ShapeDtypeStruct((n, LANES), F32),
                   jax.ShapeDtypeStruct((n, LANES), jnp.int32)],
        compiler_params=pltpu.CompilerParams(
            dimension_semantics=("arbitrary",), vmem_limit_bytes=VMEM_LIMIT),
        name="router",
    )(x, norm_w.reshape(1, d), rw, rb)
    return h, gates[:, :TOP_K], idx[:, :TOP_K]


T_EXPERT, T_FIRST, T_GROUP, T_NEXT_EXPERT, T_LAST_GROUP, T_COLS = range(6)


def _moe_block_tables(block_e, n_valid):
    nb = block_e.shape[0]
    valid = jnp.arange(nb) < n_valid
    prev = jnp.concatenate([block_e[:1] - 1, block_e[:-1]])
    first = jnp.logical_and(valid, block_e != prev)
    group = jnp.cumsum(first.astype(jnp.int32)) - 1
    n_groups = jnp.sum(first.astype(jnp.int32))
    group_e = jnp.zeros((nb,), jnp.int32).at[jnp.where(first, group, nb)].set(block_e, mode="drop")
    last = group == n_groups - 1
    nxt = jnp.where(last, block_e[0], group_e[jnp.minimum(group + 1, nb - 1)])
    tab = jnp.stack([block_e, first.astype(jnp.int32), group, nxt, last.astype(jnp.int32)], axis=1)
    meta = jnp.stack([n_valid.astype(jnp.int32), n_groups.astype(jnp.int32)])
    return tab.astype(jnp.int32).reshape(-1), meta


def _tab(tab_ref, b, col):
    return tab_ref[b * T_COLS + col]


def _weight_copy(w_hbm, buf, sem, e, j, slot, tn):
    return pltpu.make_async_copy(w_hbm.at[e, :, pl.ds(pl.multiple_of(j * tn, tn), tn)], buf.at[slot], sem.at[slot])


def _prefetch_step(tab_ref, meta_ref, weights, tn):
    j = pl.program_id(0)
    b = pl.program_id(1)
    nj = pl.num_programs(0)

    @pl.when(_tab(tab_ref, b, T_FIRST) == 1)
    def _():
        e = _tab(tab_ref, b, T_EXPERT)
        slot = (j * meta_ref[1] + _tab(tab_ref, b, T_GROUP)) % 2

        @pl.when(jnp.logical_and(j == 0, b == 0))
        def _():
            for w_hbm, buf, sem, _ in weights:
                _weight_copy(w_hbm, buf, sem, e, j, slot, tn).start()

        for w_hbm, buf, sem, _ in weights:
            _weight_copy(w_hbm, buf, sem, e, j, slot, tn).wait()

        last = _tab(tab_ref, b, T_LAST_GROUP) == 1
        j_next = jnp.where(last, j + 1, j)

        @pl.when(j_next < nj)
        def _():
            for w_hbm, buf, sem, _ in weights:
                _weight_copy(w_hbm, buf, sem, _tab(tab_ref, b, T_NEXT_EXPERT), j_next, 1 - slot, tn).start()

        for _, buf, _, wb in weights:
            wb[...] = buf[slot].astype(BF16)


def _moe_up_kernel(tab_ref, meta_ref, x_ref, wg_hbm, wu_hbm, bg_ref, bu_ref, h_ref,
                   wg_buf, wu_buf, wg_sem, wu_sem, wgb, wub, *, tn):
    b = pl.program_id(1)
    _prefetch_step(tab_ref, meta_ref, [(wg_hbm, wg_buf, wg_sem, wgb), (wu_hbm, wu_buf, wu_sem, wub)], tn)

    @pl.when(b < meta_ref[0])
    def _():
        x = x_ref[...]
        g = _dot(x, wgb[...]) + bg_ref[...]
        u = _dot(x, wub[...]) + bu_ref[...]
        gl = jnp.minimum(g, SWIGLU_LIMIT)
        lin = jnp.clip(u, -SWIGLU_LIMIT, SWIGLU_LIMIT)
        h_ref[...] = (gl * jax.nn.sigmoid(SWIGLU_ALPHA * gl) * (lin + 1.0)).astype(h_ref.dtype)

    @pl.when(b >= meta_ref[0])
    def _():
        h_ref[...] = jnp.zeros(h_ref.shape, h_ref.dtype)


def _moe_down_kernel(tab_ref, meta_ref, h_ref, wd_hbm, bd_ref, gate_ref, y_ref,
                     wd_buf, wd_sem, wdb, *, tn):
    b = pl.program_id(1)
    _prefetch_step(tab_ref, meta_ref, [(wd_hbm, wd_buf, wd_sem, wdb)], tn)

    @pl.when(b < meta_ref[0])
    def _():
        y = _dot(h_ref[...], wdb[...]) + bd_ref[...]
        y_ref[...] = y * gate_ref[...]

    @pl.when(b >= meta_ref[0])
    def _():
        y_ref[...] = jnp.zeros(y_ref.shape, y_ref.dtype)


def _moe_experts(xs, gate_rows, block_e, n_valid, w_gate, b_gate, w_up, b_up, w_down, b_down, tn_up, tn_down):
    r, d = xs.shape
    nb = r // MOE_TM
    ne, _, dff = w_gate.shape
    tab, meta = _moe_block_tables(block_e, n_valid)
    cp = pltpu.CompilerParams(dimension_semantics=("arbitrary", "arbitrary"), vmem_limit_bytes=VMEM_LIMIT)
    e_of = lambda j, b, tab, meta: (tab[b * T_COLS + T_EXPERT], 0, j)
    h = pl.pallas_call(
        functools.partial(_moe_up_kernel, tn=tn_up),
        grid_spec=pltpu.PrefetchScalarGridSpec(
            num_scalar_prefetch=2,
            grid=(dff // tn_up, nb),
            in_specs=[pl.BlockSpec((MOE_TM, d), lambda j, b, tab, meta: (b, 0)),
                      pl.BlockSpec(memory_space=pl.ANY),
                      pl.BlockSpec(memory_space=pl.ANY),
                      pl.BlockSpec((None, 1, tn_up), e_of),
                      pl.BlockSpec((None, 1, tn_up), e_of)],
            out_specs=pl.BlockSpec((MOE_TM, tn_up), lambda j, b, tab, meta: (b, j)),
            scratch_shapes=[pltpu.VMEM((2, d, tn_up), F32), pltpu.VMEM((2, d, tn_up), F32),
                            pltpu.SemaphoreType.DMA((2,)), pltpu.SemaphoreType.DMA((2,)),
                            pltpu.VMEM((d, tn_up), BF16), pltpu.VMEM((d, tn_up), BF16)]),
        out_shape=jax.ShapeDtypeStruct((r, dff), BF16),
        compiler_params=cp,
        name="moe_up",
    )(tab, meta, xs, w_gate, w_up, b_gate.reshape(ne, 1, dff), b_up.reshape(ne, 1, dff))
    return pl.pallas_call(
        functools.partial(_moe_down_kernel, tn=tn_down),
        grid_spec=pltpu.PrefetchScalarGridSpec(
            num_scalar_prefetch=2,
            grid=(d // tn_down, nb),
            in_specs=[pl.BlockSpec((MOE_TM, dff), lambda j, b, tab, meta: (b, 0)),
                      pl.BlockSpec(memory_space=pl.ANY),
                      pl.BlockSpec((None, 1, tn_down), e_of),
                      pl.BlockSpec((MOE_TM, 1), lambda j, b, tab, meta: (b, 0))],
            out_specs=pl.BlockSpec((MOE_TM, tn_down), lambda j, b, tab, meta: (b, j)),
            scratch_shapes=[pltpu.VMEM((2, dff, tn_down), F32), pltpu.SemaphoreType.DMA((2,)),
                            pltpu.VMEM((dff, tn_down), BF16)]),
        out_shape=jax.ShapeDtypeStruct((r, d), F32),
        compiler_params=cp,
        name="moe_down",
    )(tab, meta, h, w_down, b_down.reshape(ne, 1, d), gate_rows)


def _softplus(x):
    return jnp.maximum(x, 0.0) + jnp.log1p(jnp.exp(-jnp.abs(x)))


def _log_sigmoid(x):
    return -_softplus(-x)


def _softcap(x):
    return GATE_SOFTCAP * jnp.tanh(x / GATE_SOFTCAP)


def _iota2(c):
    return (lax.broadcasted_iota(jnp.int32, (c, c), 0), lax.broadcasted_iota(jnp.int32, (c, c), 1))


def _unit_lower_inverse(a, c):
    row, col = _iota2(c)
    eye = jnp.where(row == col, 1.0, 0.0).astype(F32)
    blk = min(c, 16)
    sh = int(math.log2(blk))
    same = jnp.right_shift(row, sh) == jnp.right_shift(col, sh)
    n = jnp.where(same, -a, 0.0)
    t = eye + n
    p = n
    size = 2
    while size < blk:
        p = _dot(p, p, HP)
        yield
        t = t + _dot(t, p, HP)
        yield
        size *= 2
    m = blk
    while m < c:
        s1 = int(math.log2(m))
        in_m = jnp.right_shift(row, s1) == jnp.right_shift(col, s1)
        in_2m = jnp.right_shift(row, s1 + 1) == jnp.right_shift(col, s1 + 1)
        e = jnp.where(in_2m, jnp.where(in_m, 0.0, a), 0.0)
        et = _dot(e, t, HP)
        yield
        t = t - _dot(t, et, HP)
        yield
        m *= 2
    return t


def _interleave(gens):
    results = [None] * len(gens)
    live = list(range(len(gens)))
    while live:
        nxt = []
        for i in live:
            try:
                next(gens[i])
                nxt.append(i)
            except StopIteration as stop:
                results[i] = stop.value
        live = nxt
    return results


def _gdn_kernel(pm_ref, ps_ref, pst_ref, conv_in_ref, s_in_ref, convw_ref, alog_r_ref, dtb_r_ref,
                alog_c_ref, dtb_c_ref, gnw_ref,
                out_ref, conv_out_ref, s_out_ref,
                xbuf, qkv_s, s_s, gct_s, *, c):
    n = pl.program_id(1)

    @pl.when(n == 0)
    def _():
        xbuf[HALO - (CONV_W - 1):HALO, :] = conv_in_ref[...]
        s_s[...] = s_in_ref[...]

    xbuf[HALO:HALO + c, :] = pm_ref[:, :GDN_CONV_DIM]
    y = jnp.zeros((c, GDN_CONV_DIM), F32)
    for j in range(CONV_W):
        lo = HALO - (CONV_W - 1) + j
        y = y + xbuf[lo:lo + c, :] * convw_ref[j:j + 1, :]
    xbuf[HALO - (CONV_W - 1):HALO, :] = xbuf[HALO + c - (CONV_W - 1):HALO + c, :]
    y = y * jax.nn.sigmoid(y)
    for hk in range(GDN_HK):
        q = y[:, hk * GDN_DK:(hk + 1) * GDN_DK]
        k = y[:, GDN_KEY_DIM + hk * GDN_DK:GDN_KEY_DIM + (hk + 1) * GDN_DK]
        qkv_s[:, hk * GDN_DK:(hk + 1) * GDN_DK] = (
            q * lax.rsqrt(jnp.sum(q * q, axis=-1, keepdims=True) + 1e-6) * (GDN_DK ** -0.5))
        qkv_s[:, GDN_KEY_DIM + hk * GDN_DK:GDN_KEY_DIM + (hk + 1) * GDN_DK] = (
            k * lax.rsqrt(jnp.sum(k * k, axis=-1, keepdims=True) + 1e-6))
    qkv_s[:, 2 * GDN_KEY_DIM:] = y[:, 2 * GDN_KEY_DIM:]

    ps = ps_ref[...]
    beta_all = jax.nn.sigmoid(ps)
    g_all = -jnp.exp(alog_r_ref[...]) * _softplus(ps + dtb_r_ref[...])
    row, col = _iota2(c)
    causal = row >= col
    strict = row > col
    gc_all = _dot(jnp.where(causal, 1.0, 0.0).astype(F32), g_all, HP)
    g_t = -jnp.exp(alog_c_ref[...]) * _softplus(pst_ref[...] + dtb_c_ref[...])
    gct_s[...] = _dot(g_t, jnp.where(row <= col, 1.0, 0.0).astype(F32), HP)
    lane = lax.broadcasted_iota(jnp.int32, (c, LANES), 1)
    gnw = gnw_ref[...]

    def load_head(h):
        hk = h // (GDN_HV // GDN_HK)
        q = qkv_s[:, pl.ds(pl.multiple_of(hk * GDN_DK, GDN_DK), GDN_DK)]
        k = qkv_s[:, pl.ds(pl.multiple_of(GDN_KEY_DIM + hk * GDN_DK, GDN_DK), GDN_DK)]
        v = qkv_s[:, pl.ds(pl.multiple_of(2 * GDN_KEY_DIM + h * GDN_DV, GDN_DV), GDN_DV)]
        gr = gct_s[pl.ds(LANE_DT + h, 1), :]
        z = pm_ref[:, pl.ds(pl.multiple_of(GDN_CONV_DIM + h * GDN_DV, GDN_DV), GDN_DV)]
        return q, k, v, gr, z, s_s[h]

    def compute_head(h, q, k, v, gr, z, s):
        beta = jnp.sum(jnp.where(lane == h, beta_all, 0.0), axis=-1, keepdims=True)
        gc = jnp.sum(jnp.where(lane == LANE_DT + h, gc_all, 0.0), axis=-1, keepdims=True)
        decay = jnp.where(causal, jnp.exp(jnp.where(causal, gc - gr, 0.0)), 0.0)
        kb = k * beta
        kbf = k.astype(BF16)
        a = jnp.where(strict, _dot_nt(kb.astype(BF16), kbf) * decay, 0.0)
        qk = jnp.where(causal, _dot_nt(q.astype(BF16), kbf) * decay, 0.0)
        yield
        t = yield from _unit_lower_inverse(a, c)
        eg = jnp.exp(gc)
        u = _dot(t, v * beta, HP)
        w = _dot(t, kb * eg, HP)
        yield
        sb = s.astype(BF16)
        v_new = u - _dot(w.astype(BF16), sb)
        o = _dot((q * eg).astype(BF16), sb)
        yield
        vnb = v_new.astype(BF16)
        o = o + _dot(qk.astype(BF16), vnb)
        g_last = gr[:, c - 1:c]
        k_dec = k * jnp.exp(g_last - gc)
        s_new = s * jnp.exp(g_last) + _dot_tn(k_dec.astype(BF16), vnb)
        yield
        o = o * lax.rsqrt(jnp.mean(o * o, axis=-1, keepdims=True) + RMS_EPS) * gnw
        return o * (z * jax.nn.sigmoid(z)), s_new

    def head_group(i, carry):
        hs = [i * HEAD_GROUP + r for r in range(HEAD_GROUP)]
        loaded = [load_head(h) for h in hs]
        done = _interleave([compute_head(h, *x) for h, x in zip(hs, loaded)])
        for h, (o, s_new) in zip(hs, done):
            s_s[h] = s_new
            out_ref[:, pl.ds(pl.multiple_of(h * GDN_DV, GDN_DV), GDN_DV)] = o
        return carry

    lax.fori_loop(0, GDN_HV // HEAD_GROUP, head_group, 0)

    @pl.when(n == pl.num_programs(1) - 1)
    def _():
        conv_out_ref[...] = xbuf[HALO - (CONV_W - 1):HALO, :]
        s_out_ref[...] = s_s[...]


def _gdn_mixer(pm, ps, pst, conv_in, s_in, conv_w, a_log, dt_bias, g_norm_w, *, row0, nb, nch, c):
    assert row0 % c == 0
    blk0 = row0 // c
    alog_r = jnp.zeros((1, LANES), F32).at[0, LANE_DT:LANE_DT + GDN_HV].set(a_log)
    dtb_r = jnp.zeros((1, LANES), F32).at[0, LANE_DT:LANE_DT + GDN_HV].set(dt_bias)
    rows = lambda b, n: (blk0 + b * nch + n, 0)
    const2 = lambda b, n: (0, 0)
    return pl.pallas_call(
        functools.partial(_gdn_kernel, c=c),
        grid=(nb, nch),
        in_specs=[pl.BlockSpec((c, GDN_COLS), rows),
                  pl.BlockSpec((c, LANES), rows),
                  pl.BlockSpec((None, None, LANES, c), lambda b, n: (b, n, 0, 0)),
                  pl.BlockSpec((None, CONV_W - 1, GDN_CONV_DIM), lambda b, n: (b, 0, 0)),
                  pl.BlockSpec((None, GDN_HV, GDN_DK, GDN_DV), lambda b, n: (b, 0, 0, 0)),
                  pl.BlockSpec((CONV_W, GDN_CONV_DIM), const2),
                  pl.BlockSpec((1, LANES), const2),
                  pl.BlockSpec((1, LANES), const2),
                  pl.BlockSpec((LANES, 1), const2),
                  pl.BlockSpec((LANES, 1), const2),
                  pl.BlockSpec((1, GDN_DV), const2)],
        out_specs=[pl.BlockSpec((c, GDN_VAL_DIM), lambda b, n: (b * nch + n, 0)),
                   pl.BlockSpec((None, CONV_W - 1, GDN_CONV_DIM), lambda b, n: (b, 0, 0)),
                   pl.BlockSpec((None, GDN_HV, GDN_DK, GDN_DV), lambda b, n: (b, 0, 0, 0))],
        out_shape=[jax.ShapeDtypeStruct((nb * nch * c, GDN_VAL_DIM), F32),
                   jax.ShapeDtypeStruct((nb, CONV_W - 1, GDN_CONV_DIM), F32),
                   jax.ShapeDtypeStruct((nb, GDN_HV, GDN_DK, GDN_DV), F32)],
        scratch_shapes=[pltpu.VMEM((HALO + c, GDN_CONV_DIM), F32),
                        pltpu.VMEM((c, GDN_CONV_DIM), F32),
                        pltpu.VMEM((GDN_HV, GDN_DK, GDN_DV), F32),
                        pltpu.VMEM((LANES, c), F32)],
        compiler_params=pltpu.CompilerParams(
            dimension_semantics=("arbitrary", "arbitrary"), vmem_limit_bytes=VMEM_LIMIT),
        name=f"gdn_c{c}",
    )(pm, ps, pst, conv_in, s_in, conv_w, alog_r, dtb_r, alog_r.reshape(LANES, 1), dtb_r.reshape(LANES, 1),
      g_norm_w.reshape(1, GDN_DV))


def _mlstm_kernel(pm_ref, ps_ref, pst_ref, c_in_ref, n_in_ref, m_in_ref, gb_r_ref, gb_c_ref, nw_ref,
                  out_ref, c_out_ref, n_out_ref, m_out_ref,
                  c_s, n_s, m_s, *, c):
    step = pl.program_id(1)

    @pl.when(step == 0)
    def _():
        c_s[...] = c_in_ref[...]
        n_s[...] = n_in_ref[...]
        m_s[...] = m_in_ref[...]

    row, col = _iota2(c)
    causal = row >= col
    pre_r = _softcap(ps_ref[...] + gb_r_ref[...])
    bcum_all = _dot(jnp.where(causal, 1.0, 0.0).astype(F32), _log_sigmoid(pre_r), HP)
    pre_t = _softcap(pst_ref[...] + gb_c_ref[...])
    bcum_t = _dot(_log_sigmoid(pre_t), jnp.where(row <= col, 1.0, 0.0).astype(F32), HP)

    for h in range(ML_H):
        q = pm_ref[:, h * ML_DQK:(h + 1) * ML_DQK] * (ML_DQK ** -0.5)
        k = pm_ref[:, ML_QK_DIM + h * ML_DQK:ML_QK_DIM + (h + 1) * ML_DQK]
        v = pm_ref[:, 2 * ML_QK_DIM + h * ML_DV:2 * ML_QK_DIM + (h + 1) * ML_DV]
        og = pm_ref[:, 2 * ML_QK_DIM + ML_V_DIM + h * ML_DV:2 * ML_QK_DIM + ML_V_DIM + (h + 1) * ML_DV]
        li_c = pre_r[:, LANE_I + h:LANE_I + h + 1]
        b_c = bcum_all[:, LANE_F + h:LANE_F + h + 1]
        li_r = pre_t[LANE_I + h:LANE_I + h + 1, :]
        b_r = bcum_t[LANE_F + h:LANE_F + h + 1, :]
        m_prev = m_s[:, h:h + 1]
        dlog = jnp.where(causal, b_c - b_r + li_r, -jnp.inf)
        dmax = jnp.max(dlog, axis=-1, keepdims=True)
        inter = b_c + m_prev
        mt = jnp.maximum(inter, dmax)
        qb = q.astype(BF16)
        kb16 = k.astype(BF16)
        vb = v.astype(BF16)
        pw = jnp.exp(dlog - mt) * _dot_nt(qb, kb16)
        sc_in = jnp.exp(inter - mt)
        cs = c_s[h]
        num = sc_in * _dot(qb, cs.astype(BF16)) + _dot(pw.astype(BF16), vb)
        den = sc_in * jnp.sum(q * n_s[h], axis=-1, keepdims=True) + jnp.sum(pw, axis=-1, keepdims=True)
        hh = num / jnp.maximum(jnp.abs(den), jnp.exp(-mt))
        b_last = b_r[:, c - 1:c]
        w_c = b_last - b_c + li_c
        w_r = b_last - b_r + li_r
        m_new = jnp.maximum(b_last + m_prev, jnp.max(w_r, axis=-1, keepdims=True))
        sc = jnp.exp(b_last + m_prev - m_new)
        kw = k * jnp.exp(w_c - m_new)
        c_s[h] = sc * cs + _dot_tn(kw.astype(BF16), vb)
        n_s[h] = sc * n_s[h] + jnp.sum(kw, axis=0, keepdims=True)
        m_s[:, h:h + 1] = m_new
        hn = hh * lax.rsqrt(jnp.mean(hh * hh, axis=-1, keepdims=True) + RMS_EPS) * nw_ref[:, h * ML_DV:(h + 1) * ML_DV]
        out_ref[:, h * ML_DV:(h + 1) * ML_DV] = hn * jax.nn.sigmoid(og)

    @pl.when(step == pl.num_programs(1) - 1)
    def _():
        c_out_ref[...] = c_s[...]
        n_out_ref[...] = n_s[...]
        m_out_ref[...] = m_s[...]


def _mlstm_mixer(pm, ps, pst, c_in, n_in, m_in, gate_bias, norm_w, *, row0, nb, nch, c):
    assert row0 % c == 0
    blk0 = row0 // c
    gb_r = (jnp.zeros((1, LANES), F32).at[0, LANE_I:LANE_I + ML_H].set(gate_bias[0])
            .at[0, LANE_F:LANE_F + ML_H].set(gate_bias[1]))
    rows = lambda b, n: (blk0 + b * nch + n, 0)
    const2 = lambda b, n: (0, 0)
    out, c_out, n_out, m_out = pl.pallas_call(
        functools.partial(_mlstm_kernel, c=c),
        grid=(nb, nch),
        in_specs=[pl.BlockSpec((c, ML_COLS), lambda b, n: (blk0 + b * nch + n, 1)),
                  pl.BlockSpec((c, LANES), rows),
                  pl.BlockSpec((None, None, LANES, c), lambda b, n: (b, n, 0, 0)),
                  pl.BlockSpec((None, ML_H, ML_DQK, ML_DV), lambda b, n: (b, 0, 0, 0)),
                  pl.BlockSpec((None, ML_H, 1, ML_DQK), lambda b, n: (b, 0, 0, 0)),
                  pl.BlockSpec((None, 1, ML_H), lambda b, n: (b, 0, 0)),
                  pl.BlockSpec((1, LANES), const2),
                  pl.BlockSpec((LANES, 1), const2),
                  pl.BlockSpec((1, ML_V_DIM), const2)],
        out_specs=[pl.BlockSpec((c, ML_V_DIM), lambda b, n: (b * nch + n, 0)),
                   pl.BlockSpec((None, ML_H, ML_DQK, ML_DV), lambda b, n: (b, 0, 0, 0)),
                   pl.BlockSpec((None, ML_H, 1, ML_DQK), lambda b, n: (b, 0, 0, 0)),
                   pl.BlockSpec((None, 1, ML_H), lambda b, n: (b, 0, 0))],
        out_shape=[jax.ShapeDtypeStruct((nb * nch * c, ML_V_DIM), F32),
                   jax.ShapeDtypeStruct((nb, ML_H, ML_DQK, ML_DV), F32),
                   jax.ShapeDtypeStruct((nb, ML_H, 1, ML_DQK), F32),
                   jax.ShapeDtypeStruct((nb, 1, ML_H), F32)],
        scratch_shapes=[pltpu.VMEM((ML_H, ML_DQK, ML_DV), F32),
                        pltpu.VMEM((ML_H, 1, ML_DQK), F32),
                        pltpu.VMEM((1, ML_H), F32)],
        compiler_params=pltpu.CompilerParams(
            dimension_semantics=("arbitrary", "arbitrary"), vmem_limit_bytes=VMEM_LIMIT),
        name=f"mlstm_c{c}",
    )(pm, ps, pst, c_in, n_in.reshape(nb, ML_H, 1, ML_DQK), m_in.reshape(nb, 1, ML_H),
      gb_r, gb_r.reshape(LANES, 1), norm_w.reshape(1, ML_V_DIM))
    return out, c_out, n_out.reshape(nb, ML_H, ML_DQK), m_out.reshape(nb, ML_H)


def _mix_rows(pm, ps, state, p, *, row0, nb, nch, c):
    conv_w, a_log, dt_bias, g_norm_w, ml_gb, ml_nw = p
    conv_in, s_in, c_in, n_in, m_in = state
    n_rows = nb * nch * c
    pst = jnp.swapaxes(ps[row0:row0 + n_rows].reshape(nb, nch, c, LANES), 2, 3)
    og, conv_out, s_out = _gdn_mixer(pm, ps, pst, conv_in, s_in, conv_w, a_log, dt_bias, g_norm_w,
                                     row0=row0, nb=nb, nch=nch, c=c)
    om, c_out, n_out, m_out = _mlstm_mixer(pm, ps, pst, c_in, n_in, m_in, ml_gb, ml_nw,
                                           row0=row0, nb=nb, nch=nch, c=c)
    return jnp.concatenate([og, om], axis=1), (conv_out, s_out, c_out, n_out, m_out)


def kernel(x_prompt, x_sample, state_gdn_conv, state_gdn, state_mlstm_c, state_mlstm_n, state_mlstm_m,
           meta_tokens, norm_mix_w, w_in, gdn_conv_w, gdn_a_log, gdn_dt_bias, gdn_norm_w,
           ml_gate_bias, ml_norm_w, w_out, norm_ffn_w, router_w, router_b,
           w_gate, b_gate, w_up, b_up, w_down, b_down, norm_final_w):
    B, S, D = x_prompt.shape
    BS, TS, _ = x_sample.shape
    n_main = B * S
    n_meta = B * N_META
    row_meta = n_main
    row_samp = n_main + n_meta
    n_tok = row_samp + BS * TS
    meta = jnp.broadcast_to(meta_tokens.astype(x_prompt.dtype), (B, N_META, D))
    x_all = jnp.concatenate([x_prompt.reshape(n_main, D), meta.reshape(n_meta, D),
                             x_sample.reshape(BS * TS, D)], axis=0)

    w = w_in[0]
    offs = [0]
    for s in IN_SPLITS:
        offs.append(offs[-1] + s)
    col = lambda i: w[:, offs[i]:offs[i + 1]]
    w_main = jnp.concatenate([col(0), col(1), col(2), col(3), col(6), col(7), col(8), col(9)], axis=1)
    w_small = jnp.concatenate([col(4), col(5), col(10), col(11)], axis=1)
    w_small = jnp.pad(w_small, ((0, 0), (0, LANES - w_small.shape[1])))

    tm = n_tok // 10
    h = _rmsnorm(x_all, norm_mix_w[0], BF16, n_tok // 20)
    proj_main = _matmul(h, w_main, tm, 512)
    proj_small = _matmul(h, w_small, tm, LANES)

    mix_p = (gdn_conv_w[0], gdn_a_log[0], gdn_dt_bias[0], gdn_norm_w[0], ml_gate_bias[0], ml_norm_w[0])
    p_init = (jnp.zeros((B, CONV_W - 1, GDN_CONV_DIM), F32),
              jnp.zeros((B, GDN_HV, GDN_DK, GDN_DV), F32),
              jnp.zeros((B, ML_H, ML_DQK, ML_DV), F32),
              jnp.zeros((B, ML_H, ML_DQK), F32),
              jnp.zeros((B, ML_H), F32))
    mixed_meta, st = _mix_rows(proj_main, proj_small, p_init, mix_p, row0=row_meta, nb=B, nch=1, c=N_META)
    c_main = math.gcd(S, GDN_CHUNK)
    mixed_main, p_st = _mix_rows(proj_main, proj_small, st, mix_p, row0=0, nb=B, nch=S // c_main, c=c_main)
    s_init = (state_gdn_conv[0], state_gdn[0], state_mlstm_c[0], state_mlstm_n[0], state_mlstm_m[0])
    c_samp = math.gcd(TS, GDN_CHUNK)
    mixed_samp, s_st = _mix_rows(proj_main, proj_small, s_init, mix_p, row0=row_samp, nb=BS,
                                 nch=TS // c_samp, c=c_samp)
    mixed = jnp.concatenate([mixed_main, mixed_meta, mixed_samp], axis=0).astype(BF16)

    x1 = _matmul(mixed, w_out[0], tm, 512, residual=x_all)

    h2, gates, top_idx = _router(x1, norm_ffn_w[0], router_w[0], router_b[0], n_tok // 20)
    n_assign = n_tok * TOP_K
    flat_e = top_idx.reshape(n_assign)
    order = jnp.argsort(flat_e)
    e_sorted = flat_e[order]
    tok_sorted = (order // TOP_K).astype(jnp.int32)
    counts = jnp.bincount(flat_e, length=N_EXPERTS)
    padded = (counts + MOE_TM - 1) // MOE_TM * MOE_TM
    start = jnp.cumsum(counts) - counts
    pad_end = jnp.cumsum(padded)
    pad_start = pad_end - padded
    dest = (pad_start[e_sorted] + jnp.arange(n_assign) - start[e_sorted]).astype(jnp.int32)
    n_blocks = -(-n_assign // MOE_TM) + N_EXPERTS
    n_rows = n_blocks * MOE_TM
    block_e = jnp.minimum(jnp.searchsorted(pad_end, jnp.arange(n_blocks) * MOE_TM, side='right'),
                          N_EXPERTS - 1).astype(jnp.int32)
    n_valid = (pad_end[-1] // MOE_TM).astype(jnp.int32)
    row_tok = jnp.zeros((n_rows,), jnp.int32).at[dest].set(tok_sorted)
    gate_rows = jnp.zeros((n_rows, 1), F32).at[dest, 0].set(gates.reshape(n_assign)[order])
    xs = h2[row_tok]
    y = _moe_experts(xs, gate_rows, block_e, n_valid, w_gate[0], b_gate[0], w_up[0], b_up[0],
                     w_down[0], b_down[0], 512, 1024)
    dest_by_assign = jnp.zeros((n_assign,), jnp.int32).at[order].set(dest)
    x2 = x1 + y[dest_by_assign].reshape(n_tok, TOP_K, D).sum(axis=1)

    y_all = _rmsnorm(x2, norm_final_w, F32, n_tok // 20)
    y_prompt = y_all[:n_main].reshape(B, S, D)
    y_sample = y_all[row_samp:].reshape(BS, TS, D)

    likes = (state_gdn_conv, state_gdn, state_mlstm_c, state_mlstm_n, state_mlstm_m)
    p_out = tuple(a[None].astype(l.dtype) for a, l in zip(p_st, likes))
    s_out = tuple(a[None].astype(l.dtype) for a, l in zip(s_st, likes))
    return (y_prompt, y_sample) + p_out + s_out
```

```python
import functools
import math

import jax
import jax.numpy as jnp
from jax import lax
from jax.experimental import pallas as pl
from jax.experimental.pallas import tpu as pltpu

F32 = jnp.float32
BF16 = jnp.bfloat16
HP = lax.Precision.HIGHEST

D_MODEL = 4096
N_META = 16
RMS_EPS = 1e-6
GDN_DK = 128
GDN_DV = 128
GDN_HV = 16
GDN_HK = 8
GDN_KEY_DIM = GDN_HK * GDN_DK
GDN_VAL_DIM = GDN_HV * GDN_DV
GDN_CONV_DIM = 2 * GDN_KEY_DIM + GDN_VAL_DIM
CONV_W = 4
GDN_CHUNK = 64
ML_DQK = 256
ML_DV = 512
ML_H = 4
ML_QK_DIM = ML_H * ML_DQK
ML_V_DIM = ML_H * ML_DV
ML_CHUNK = 64
GATE_SOFTCAP = 15.0
IN_SPLITS = (GDN_KEY_DIM, GDN_KEY_DIM, GDN_VAL_DIM, GDN_VAL_DIM, GDN_HV, GDN_HV,
             ML_QK_DIM, ML_QK_DIM, ML_V_DIM, ML_V_DIM, ML_H, ML_H)
N_EXPERTS = 32
TOP_K = 4
SWIGLU_LIMIT = 7.0
SWIGLU_ALPHA = 1.702

LANES = 128
VMEM_LIMIT = 56 * 1024 * 1024
MOE_TM = 256
NEG_BIG = -1e30
COMBINE_TB = 32
HALO = 8
GDN_COLS = GDN_CONV_DIM + GDN_VAL_DIM
ML_COLS = 2 * ML_QK_DIM + 2 * ML_V_DIM
LANE_DT = GDN_HV
LANE_I = 2 * GDN_HV
LANE_F = 2 * GDN_HV + ML_H


def _dot(a, b, precision=None):
    return jnp.dot(a, b, preferred_element_type=F32, precision=precision)


def _split_bf16(a):
    hi = a.astype(BF16)
    return hi, (a - hi.astype(F32)).astype(BF16)


def _dot3(a, b):
    ah, al = _split_bf16(a)
    bh, bl = _split_bf16(b)
    return _dot(ah, bh) + _dot(al, bh) + _dot(ah, bl)


def _dot_nt(a, b):
    return lax.dot_general(a, b, (((1,), (1,)), ((), ())), preferred_element_type=F32)


def _dot_tn(a, b):
    return lax.dot_general(a, b, (((0,), (0,)), ((), ())), preferred_element_type=F32)


def _rmsnorm_kernel(x_ref, w_ref, o_ref):
    x = x_ref[...]
    y = x * lax.rsqrt(jnp.mean(x * x, axis=-1, keepdims=True) + RMS_EPS)
    o_ref[...] = (y * w_ref[...]).astype(o_ref.dtype)


def _rmsnorm(x, w, out_dtype, tr):
    n, d = x.shape
    return pl.pallas_call(
        _rmsnorm_kernel,
        grid=(n // tr,),
        in_specs=[pl.BlockSpec((tr, d), lambda i: (i, 0)),
                  pl.BlockSpec((1, d), lambda i: (0, 0))],
        out_specs=pl.BlockSpec((tr, d), lambda i: (i, 0)),
        out_shape=jax.ShapeDtypeStruct((n, d), out_dtype),
        compiler_params=pltpu.CompilerParams(
            dimension_semantics=("arbitrary",), vmem_limit_bytes=VMEM_LIMIT),
        name="rmsnorm",
    )(x, w.reshape(1, d))


def _mm_kernel(x_ref, w_ref, o_ref, wb_ref):
    @pl.when(pl.program_id(1) == 0)
    def _():
        wb_ref[...] = w_ref[...].astype(BF16)

    o_ref[...] = _dot(x_ref[...], wb_ref[...])


def _mm_res_kernel(x_ref, w_ref, r_ref, o_ref, wb_ref):
    @pl.when(pl.program_id(1) == 0)
    def _():
        wb_ref[...] = w_ref[...].astype(BF16)

    o_ref[...] = r_ref[...] + _dot(x_ref[...], wb_ref[...])


def _matmul(x, w, tm, tn, residual=None):
    m, k = x.shape
    n = w.shape[1]
    in_specs = [pl.BlockSpec((tm, k), lambda j, i: (i, 0)),
                pl.BlockSpec((k, tn), lambda j, i: (0, j))]
    args = [x, w]
    body = _mm_kernel
    if residual is not None:
        in_specs.append(pl.BlockSpec((tm, tn), lambda j, i: (i, j)))
        args.append(residual)
        body = _mm_res_kernel
    return pl.pallas_call(
        body,
        grid=(n // tn, m // tm),
        in_specs=in_specs,
        out_specs=pl.BlockSpec((tm, tn), lambda j, i: (i, j)),
        out_shape=jax.ShapeDtypeStruct((m, n), F32),
        scratch_shapes=[pltpu.VMEM((k, tn), BF16)],
        compiler_params=pltpu.CompilerParams(
            dimension_semantics=("arbitrary", "arbitrary"), vmem_limit_bytes=VMEM_LIMIT),
        name="matmul",
    )(*args)


def _router_kernel(x_ref, nw_ref, rw_ref, rb_ref, h_ref, gate_ref, idx_ref):
    x = x_ref[...]
    h = x * lax.rsqrt(jnp.mean(x * x, axis=-1, keepdims=True) + RMS_EPS) * nw_ref[...]
    h_ref[...] = h.astype(h_ref.dtype)
    logits = _dot(h, rw_ref[...], HP) + rb_ref[...]
    lane = lax.broadcasted_iota(jnp.int32, logits.shape, 1)
    vals, idxs = [], []
    cur = logits
    for _ in range(TOP_K):
        mx = jnp.max(cur, axis=-1, keepdims=True)
        ix = jnp.min(jnp.where(cur == mx, lane, LANES), axis=-1, keepdims=True)
        vals.append(mx)
        idxs.append(ix)
        cur = jnp.where(lane == ix, NEG_BIG, cur)
    exps = [jnp.exp(v - vals[0]) for v in vals]
    denom = exps[0] + exps[1] + exps[2] + exps[3]
    gate_out = jnp.zeros(logits.shape, F32)
    idx_out = jnp.zeros(logits.shape, jnp.int32)
    for k in range(TOP_K):
        gate_out = jnp.where(lane == k, exps[k] / denom, gate_out)
        idx_out = jnp.where(lane == k, idxs[k], idx_out)
    gate_ref[...] = gate_out
    idx_ref[...] = idx_out


def _router(x, norm_w, router_w, router_b, tr):
    n, d = x.shape
    rw = jnp.zeros((d, LANES), F32).at[:, :N_EXPERTS].set(router_w)
    rb = jnp.full((1, LANES), NEG_BIG, F32).at[0, :N_EXPERTS].set(router_b)
    h, gates, idx = pl.pallas_call(
        _router_kernel,
        grid=(n // tr,),
        in_specs=[pl.BlockSpec((tr, d), lambda i: (i, 0)),
                  pl.BlockSpec((1, d), lambda i: (0, 0)),
                  pl.BlockSpec((d, LANES), lambda i: (0, 0)),
                  pl.BlockSpec((1, LANES), lambda i: (0, 0))],
        out_specs=[pl.BlockSpec((tr, d), lambda i: (i, 0)),
                   pl.BlockSpec((tr, LANES), lambda i: (i, 0)),
                   pl.BlockSpec((tr, LANES), lambda i: (i, 0))],
        out_shape=[jax.ShapeDtypeStruct((n, d), BF16),
                   jax.ShapeDtypeStruct((n, LANES), F32),
                   jax.ShapeDtypeStruct((n, LANES), jnp.int32)],
        compiler_params=pltpu.CompilerParams(
            dimension_semantics=("arbitrary",), vmem_limit_bytes=VMEM_LIMIT),
        name="router",
    )(x, norm_w.reshape(1, d), rw, rb)
    return h, gates, idx


T_EXPERT, T_FIRST, T_GROUP, T_NEXT_EXPERT, T_LAST_GROUP, T_COLS = range(6)


def _moe_block_tables(block_e, n_valid):
    nb = block_e.shape[0]
    valid = jnp.arange(nb) < n_valid
    prev = jnp.concatenate([block_e[:1] - 1, block_e[:-1]])
    first = jnp.logical_and(valid, block_e != prev)
    group = jnp.cumsum(first.astype(jnp.int32)) - 1
    n_groups = jnp.sum(first.astype(jnp.int32))
    group_e = jnp.zeros((nb,), jnp.int32).at[jnp.where(first, group, nb)].set(block_e, mode="drop")
    last = group == n_groups - 1
    nxt = jnp.where(last, block_e[0], group_e[jnp.minimum(group + 1, nb - 1)])
    tab = jnp.stack([block_e, first.astype(jnp.int32), group, nxt, last.astype(jnp.int32)], axis=1)
    meta = jnp.stack([n_valid.astype(jnp.int32), n_groups.astype(jnp.int32)])
    return tab.astype(jnp.int32).reshape(-1), meta


def _tab(tab_ref, b, col):
    return tab_ref[b * T_COLS + col]


def _weight_copy(w_hbm, buf, sem, e, j, slot, tn):
    return pltpu.make_async_copy(w_hbm.at[e, :, pl.ds(pl.multiple_of(j * tn, tn), tn)], buf.at[slot], sem.at[slot])


def _prefetch_step(tab_ref, meta_ref, weights, tn):
    j = pl.program_id(0)
    b = pl.program_id(1)
    nj = pl.num_programs(0)

    @pl.when(_tab(tab_ref, b, T_FIRST) == 1)
    def _():
        e = _tab(tab_ref, b, T_EXPERT)
        slot = (j * meta_ref[1] + _tab(tab_ref, b, T_GROUP)) % 2

        @pl.when(jnp.logical_and(j == 0, b == 0))
        def _():
            for w_hbm, buf, sem, _ in weights:
                _weight_copy(w_hbm, buf, sem, e, j, slot, tn).start()

        for w_hbm, buf, sem, _ in weights:
            _weight_copy(w_hbm, buf, sem, e, j, slot, tn).wait()

        last = _tab(tab_ref, b, T_LAST_GROUP) == 1
        j_next = jnp.where(last, j + 1, j)

        @pl.when(j_next < nj)
        def _():
            for w_hbm, buf, sem, _ in weights:
                _weight_copy(w_hbm, buf, sem, _tab(tab_ref, b, T_NEXT_EXPERT), j_next, 1 - slot, tn).start()

        for _, buf, _, wb in weights:
            wb[...] = buf[slot].astype(BF16)


def _moe_up_kernel(tab_ref, meta_ref, x_ref, wg_hbm, wu_hbm, bg_ref, bu_ref, h_ref,
                   wg_buf, wu_buf, wg_sem, wu_sem, wgb, wub, *, tn):
    b = pl.program_id(1)
    _prefetch_step(tab_ref, meta_ref, [(wg_hbm, wg_buf, wg_sem, wgb), (wu_hbm, wu_buf, wu_sem, wub)], tn)

    @pl.when(b < meta_ref[0])
    def _():
        x = x_ref[...]
        g = _dot(x, wgb[...]) + bg_ref[...]
        u = _dot(x, wub[...]) + bu_ref[...]
        gl = jnp.minimum(g, SWIGLU_LIMIT)
        lin = jnp.clip(u, -SWIGLU_LIMIT, SWIGLU_LIMIT)
        h_ref[...] = (gl * jax.nn.sigmoid(SWIGLU_ALPHA * gl) * (lin + 1.0)).astype(h_ref.dtype)

    @pl.when(b >= meta_ref[0])
    def _():
        h_ref[...] = jnp.zeros(h_ref.shape, h_ref.dtype)


def _moe_down_kernel(tab_ref, meta_ref, h_ref, wd_hbm, bd_ref, y_ref,
                     wd_buf, wd_sem, wdb, *, tn):
    b = pl.program_id(1)
    _prefetch_step(tab_ref, meta_ref, [(wd_hbm, wd_buf, wd_sem, wdb)], tn)

    @pl.when(b < meta_ref[0])
    def _():
        y_ref[...] = _dot(h_ref[...], wdb[...]) + bd_ref[...]

    @pl.when(b >= meta_ref[0])
    def _():
        y_ref[...] = jnp.zeros(y_ref.shape, y_ref.dtype)


def _moe_experts(xs, block_e, n_valid, w_gate, b_gate, w_up, b_up, w_down, b_down, tn_up, tn_down):
    r, d = xs.shape
    nb = r // MOE_TM
    ne, _, dff = w_gate.shape
    tab, meta = _moe_block_tables(block_e, n_valid)
    cp = pltpu.CompilerParams(dimension_semantics=("arbitrary", "arbitrary"), vmem_limit_bytes=VMEM_LIMIT)
    e_of = lambda j, b, tab, meta: (tab[b * T_COLS + T_EXPERT], 0, j)
    h = pl.pallas_call(
        functools.partial(_moe_up_kernel, tn=tn_up),
        grid_spec=pltpu.PrefetchScalarGridSpec(
            num_scalar_prefetch=2,
            grid=(dff // tn_up, nb),
            in_specs=[pl.BlockSpec((MOE_TM, d), lambda j, b, tab, meta: (b, 0)),
                      pl.BlockSpec(memory_space=pl.ANY),
                      pl.BlockSpec(memory_space=pl.ANY),
                      pl.BlockSpec((None, 1, tn_up), e_of),
                      pl.BlockSpec((None, 1, tn_up), e_of)],
            out_specs=pl.BlockSpec((MOE_TM, tn_up), lambda j, b, tab, meta: (b, j)),
            scratch_shapes=[pltpu.VMEM((2, d, tn_up), F32), pltpu.VMEM((2, d, tn_up), F32),
                            pltpu.SemaphoreType.DMA((2,)), pltpu.SemaphoreType.DMA((2,)),
                            pltpu.VMEM((d, tn_up), BF16), pltpu.VMEM((d, tn_up), BF16)]),
        out_shape=jax.ShapeDtypeStruct((r, dff), BF16),
        compiler_params=cp,
        name="moe_up",
    )(tab, meta, xs, w_gate, w_up, b_gate.reshape(ne, 1, dff), b_up.reshape(ne, 1, dff))
    return pl.pallas_call(
        functools.partial(_moe_down_kernel, tn=tn_down),
        grid_spec=pltpu.PrefetchScalarGridSpec(
            num_scalar_prefetch=2,
            grid=(d // tn_down, nb),
            in_specs=[pl.BlockSpec((MOE_TM, dff), lambda j, b, tab, meta: (b, 0)),
                      pl.BlockSpec(memory_space=pl.ANY),
                      pl.BlockSpec((None, 1, tn_down), e_of)],
            out_specs=pl.BlockSpec((MOE_TM, tn_down), lambda j, b, tab, meta: (b, j)),
            scratch_shapes=[pltpu.VMEM((2, dff, tn_down), F32), pltpu.SemaphoreType.DMA((2,)),
                            pltpu.VMEM((dff, tn_down), BF16)]),
        out_shape=jax.ShapeDtypeStruct((r, d), F32),
        compiler_params=cp,
        name="moe_down",
    )(tab, meta, h, w_down, b_down.reshape(ne, 1, d))


def _rank_kernel(idx_ref, rank_ref, cnt_ref, carry):
    @pl.when(pl.program_id(0) == 0)
    def _():
        carry[...] = jnp.zeros(carry.shape, F32)

    idx = idx_ref[...]
    tb = idx.shape[0]
    lane = lax.broadcasted_iota(jnp.int32, idx.shape, 1)
    hits = [lane == idx[:, k:k + 1] for k in range(TOP_K)]
    picked = sum(jnp.where(hit, 1.0, 0.0) for hit in hits)
    row = lax.broadcasted_iota(jnp.int32, (tb, tb), 0)
    col = lax.broadcasted_iota(jnp.int32, (tb, tb), 1)
    before = jnp.where(row > col, 1.0, 0.0).astype(BF16)
    seen = _dot(before, picked.astype(BF16)) + carry[...]
    rank = jnp.zeros(idx.shape, F32)
    for k in range(TOP_K):
        rk = jnp.sum(jnp.where(hits[k], seen, 0.0), axis=-1, keepdims=True)
        rank = jnp.where(lane == k, rk, rank)
    rank_ref[...] = rank.astype(jnp.int32)
    carry[...] = carry[...] + jnp.sum(picked, axis=0, keepdims=True)
    cnt_ref[...] = carry[...].astype(jnp.int32)


def _expert_ranks(top_idx_lanes, tb):
    n = top_idx_lanes.shape[0]
    return pl.pallas_call(
        _rank_kernel,
        grid=(n // tb,),
        in_specs=[pl.BlockSpec((tb, LANES), lambda i: (i, 0))],
        out_specs=[pl.BlockSpec((tb, LANES), lambda i: (i, 0)),
                   pl.BlockSpec((1, LANES), lambda i: (0, 0))],
        out_shape=[jax.ShapeDtypeStruct((n, LANES), jnp.int32),
                   jax.ShapeDtypeStruct((1, LANES), jnp.int32)],
        scratch_shapes=[pltpu.VMEM((1, LANES), F32)],
        compiler_params=pltpu.CompilerParams(dimension_semantics=("arbitrary",)),
        name="expert_ranks",
    )(top_idx_lanes)


def _row_copy(y_hbm, ybuf, sem, src_row, slot, k, r):
    return pltpu.make_async_copy(y_hbm.at[pl.ds(src_row, 1), :], ybuf.at[slot, k, pl.ds(r, 1), :], sem.at[slot])


def _combine_kernel(dest_ref, x_ref, gate_ref, w_ref, y_hbm, o_ref, ybuf, sem, *, tb):
    i = pl.program_id(0)
    n = pl.num_programs(0)

    def start_block(blk, slot):
        def body(r, carry):
            base = (blk * tb + r) * TOP_K
            for k in range(TOP_K):
                _row_copy(y_hbm, ybuf, sem, dest_ref[base + k], slot, k, r).start()
            return carry
        lax.fori_loop(0, tb, body, 0)

    def wait_block(slot):
        def body(r, carry):
            for k in range(TOP_K):
                _row_copy(y_hbm, ybuf, sem, 0, slot, k, r).wait()
            return carry
        lax.fori_loop(0, tb, body, 0)

    slot = i % 2

    @pl.when(i == 0)
    def _():
        start_block(0, 0)

    @pl.when(i + 1 < n)
    def _():
        start_block(i + 1, 1 - slot)

    wait_block(slot)
    gate = gate_ref[...]
    acc = x_ref[...]
    for k in range(TOP_K):
        acc = acc + gate[:, k:k + 1] * ybuf[slot, k]
    o_ref[...] = acc * lax.rsqrt(jnp.mean(acc * acc, axis=-1, keepdims=True) + RMS_EPS) * w_ref[...]


def _combine_norm(x, gates_lanes, dest, y, norm_w, tb):
    n, d = x.shape
    return pl.pallas_call(
        functools.partial(_combine_kernel, tb=tb),
        grid_spec=pltpu.PrefetchScalarGridSpec(
            num_scalar_prefetch=1,
            grid=(n // tb,),
            in_specs=[pl.BlockSpec((tb, d), lambda i, dest: (i, 0)),
                      pl.BlockSpec((tb, LANES), lambda i, dest: (i, 0)),
                      pl.BlockSpec((1, d), lambda i, dest: (0, 0)),
                      pl.BlockSpec(memory_space=pl.ANY)],
            out_specs=pl.BlockSpec((tb, d), lambda i, dest: (i, 0)),
            scratch_shapes=[pltpu.VMEM((2, TOP_K, tb, d), F32), pltpu.SemaphoreType.DMA((2,))]),
        out_shape=jax.ShapeDtypeStruct((n, d), F32),
        compiler_params=pltpu.CompilerParams(dimension_semantics=("arbitrary",), vmem_limit_bytes=VMEM_LIMIT),
        name="combine_norm",
    )(dest, x, gates_lanes, norm_w.reshape(1, d), y)


def _softplus(x):
    return jnp.maximum(x, 0.0) + jnp.log1p(jnp.exp(-jnp.abs(x)))


def _log_sigmoid(x):
    return -_softplus(-x)


def _softcap(x):
    return GATE_SOFTCAP * jnp.tanh(x / GATE_SOFTCAP)


def _iota2(c):
    return (lax.broadcasted_iota(jnp.int32, (c, c), 0), lax.broadcasted_iota(jnp.int32, (c, c), 1))


def _unit_lower_inverse(a, c):
    row, col = _iota2(c)
    eye = jnp.where(row == col, 1.0, 0.0).astype(F32)
    blk = min(c, 16)
    sh = int(math.log2(blk))
    same = jnp.right_shift(row, sh) == jnp.right_shift(col, sh)
    n = jnp.where(same, -a, 0.0)
    t = eye + n
    p = n
    size = 2
    while size < blk:
        p = _dot3(p, p)
        yield
        t = t + _dot3(t, p)
        yield
        size *= 2
    m = blk
    while m < c:
        s1 = int(math.log2(m))
        in_m = jnp.right_shift(row, s1) == jnp.right_shift(col, s1)
        in_2m = jnp.right_shift(row, s1 + 1) == jnp.right_shift(col, s1 + 1)
        e = jnp.where(in_2m, jnp.where(in_m, 0.0, a), 0.0)
        et = _dot3(e, t)
        yield
        t = t - _dot3(t, et)
        yield
        m *= 2
    return t


def _interleave(gens):
    results = [None] * len(gens)
    live = list(range(len(gens)))
    while live:
        nxt = []
        for i in live:
            try:
                next(gens[i])
                nxt.append(i)
            except StopIteration as stop:
                results[i] = stop.value
        live = nxt
    return results


def _gdn_kernel(pm_ref, ps_ref, pst_ref, conv_in_ref, s_in_ref, convw_ref, alog_r_ref, dtb_r_ref,
                alog_c_ref, dtb_c_ref, gnw_ref,
                out_ref, conv_out_ref, s_out_ref,
                xbuf, qkv_s, s_s, gct_s, *, c):
    n = pl.program_id(1)

    @pl.when(n == 0)
    def _():
        xbuf[HALO - (CONV_W - 1):HALO, :] = conv_in_ref[...]
        s_s[...] = s_in_ref[...]

    xbuf[HALO:HALO + c, :] = pm_ref[:, :GDN_CONV_DIM]
    y = jnp.zeros((c, GDN_CONV_DIM), F32)
    for j in range(CONV_W):
        lo = HALO - (CONV_W - 1) + j
        y = y + xbuf[lo:lo + c, :] * convw_ref[j:j + 1, :]
    xbuf[HALO - (CONV_W - 1):HALO, :] = xbuf[HALO + c - (CONV_W - 1):HALO + c, :]
    y = y * jax.nn.sigmoid(y)
    for hk in range(GDN_HK):
        q = y[:, hk * GDN_DK:(hk + 1) * GDN_DK]
        k = y[:, GDN_KEY_DIM + hk * GDN_DK:GDN_KEY_DIM + (hk + 1) * GDN_DK]
        qkv_s[:, hk * GDN_DK:(hk + 1) * GDN_DK] = (
            q * lax.rsqrt(jnp.sum(q * q, axis=-1, keepdims=True) + 1e-6) * (GDN_DK ** -0.5))
        qkv_s[:, GDN_KEY_DIM + hk * GDN_DK:GDN_KEY_DIM + (hk + 1) * GDN_DK] = (
            k * lax.rsqrt(jnp.sum(k * k, axis=-1, keepdims=True) + 1e-6))
    qkv_s[:, 2 * GDN_KEY_DIM:] = y[:, 2 * GDN_KEY_DIM:]

    ps = ps_ref[...]
    beta_all = jax.nn.sigmoid(ps)
    g_all = -jnp.exp(alog_r_ref[...]) * _softplus(ps + dtb_r_ref[...])
    row, col = _iota2(c)
    causal = row >= col
    strict = row > col
    gc_all = _dot(jnp.where(causal, 1.0, 0.0).astype(F32), g_all, HP)
    g_t = -jnp.exp(alog_c_ref[...]) * _softplus(pst_ref[...] + dtb_c_ref[...])
    gct_s[...] = _dot(g_t, jnp.where(row <= col, 1.0, 0.0).astype(F32), HP)
    lane = lax.broadcasted_iota(jnp.int32, (c, LANES), 1)
    gnw = gnw_ref[...]

    def load_head(h):
        hk = h // (GDN_HV // GDN_HK)
        q = qkv_s[:, pl.ds(pl.multiple_of(hk * GDN_DK, GDN_DK), GDN_DK)]
        k = qkv_s[:, pl.ds(pl.multiple_of(GDN_KEY_DIM + hk * GDN_DK, GDN_DK), GDN_DK)]
        v = qkv_s[:, pl.ds(pl.multiple_of(2 * GDN_KEY_DIM + h * GDN_DV, GDN_DV), GDN_DV)]
        gr = gct_s[pl.ds(LANE_DT + h, 1), :]
        z = pm_ref[:, pl.ds(pl.multiple_of(GDN_CONV_DIM + h * GDN_DV, GDN_DV), GDN_DV)]
        return q, k, v, gr, z, s_s[h]

    def compute_head(h, q, k, v, gr, z, s):
        beta = jnp.sum(jnp.where(lane == h, beta_all, 0.0), axis=-1, keepdims=True)
        gc = jnp.sum(jnp.where(lane == LANE_DT + h, gc_all, 0.0), axis=-1, keepdims=True)
        decay = jnp.where(causal, jnp.exp(jnp.where(causal, gc - gr, 0.0)), 0.0)
        kb = k * beta
        kbf = k.astype(BF16)
        a = jnp.where(strict, _dot_nt(kb.astype(BF16), kbf) * decay, 0.0)
        qk = jnp.where(causal, _dot_nt(q.astype(BF16), kbf) * decay, 0.0)
        yield
        t = yield from _unit_lower_inverse(a, c)
        eg = jnp.exp(gc)
        u = _dot3(t, v * beta)
        w = _dot3(t, kb * eg)
        yield
        sb = s.astype(BF16)
        v_new = u - _dot(w.astype(BF16), sb)
        o = _dot((q * eg).astype(BF16), sb)
        yield
        vnb = v_new.astype(BF16)
        o = o + _dot(qk.astype(BF16), vnb)
        g_last = gr[:, c - 1:c]
        k_dec = k * jnp.exp(g_last - gc)
        s_new = s * jnp.exp(g_last) + _dot_tn(k_dec.astype(BF16), vnb)
        yield
        o = o * lax.rsqrt(jnp.mean(o * o, axis=-1, keepdims=True) + RMS_EPS) * gnw
        return o * (z * jax.nn.sigmoid(z)), s_new

    group = GDN_HV // 2 if c >= GDN_CHUNK else GDN_HV

    def head_group(i, carry):
        hs = [i * group + r for r in range(group)]
        loaded = [load_head(h) for h in hs]
        done = _interleave([compute_head(h, *x) for h, x in zip(hs, loaded)])
        for h, (o, s_new) in zip(hs, done):
            s_s[h] = s_new
            out_ref[:, pl.ds(pl.multiple_of(h * GDN_DV, GDN_DV), GDN_DV)] = o
        return carry

    lax.fori_loop(0, GDN_HV // group, head_group, 0)

    @pl.when(n == pl.num_programs(1) - 1)
    def _():
        conv_out_ref[...] = xbuf[HALO - (CONV_W - 1):HALO, :]
        s_out_ref[...] = s_s[...]


def _gdn_mixer(pm, ps, pst, conv_in, s_in, conv_w, a_log, dt_bias, g_norm_w, *, row0, nb, nch, c):
    assert row0 % c == 0
    blk0 = row0 // c
    alog_r = jnp.zeros((1, LANES), F32).at[0, LANE_DT:LANE_DT + GDN_HV].set(a_log)
    dtb_r = jnp.zeros((1, LANES), F32).at[0, LANE_DT:LANE_DT + GDN_HV].set(dt_bias)
    rows = lambda b, n: (blk0 + b * nch + n, 0)
    const2 = lambda b, n: (0, 0)
    return pl.pallas_call(
        functools.partial(_gdn_kernel, c=c),
        grid=(nb, nch),
        in_specs=[pl.BlockSpec((c, GDN_COLS), rows),
                  pl.BlockSpec((c, LANES), rows),
                  pl.BlockSpec((None, None, LANES, c), lambda b, n: (b, n, 0, 0)),
                  pl.BlockSpec((None, CONV_W - 1, GDN_CONV_DIM), lambda b, n: (b, 0, 0)),
                  pl.BlockSpec((None, GDN_HV, GDN_DK, GDN_DV), lambda b, n: (b, 0, 0, 0)),
                  pl.BlockSpec((CONV_W, GDN_CONV_DIM), const2),
                  pl.BlockSpec((1, LANES), const2),
                  pl.BlockSpec((1, LANES), const2),
                  pl.BlockSpec((LANES, 1), const2),
                  pl.BlockSpec((LANES, 1), const2),
                  pl.BlockSpec((1, GDN_DV), const2)],
        out_specs=[pl.BlockSpec((c, GDN_VAL_DIM), lambda b, n: (b * nch + n, 0)),
                   pl.BlockSpec((None, CONV_W - 1, GDN_CONV_DIM), lambda b, n: (b, 0, 0)),
                   pl.BlockSpec((None, GDN_HV, GDN_DK, GDN_DV), lambda b, n: (b, 0, 0, 0))],
        out_shape=[jax.ShapeDtypeStruct((nb * nch * c, GDN_VAL_DIM), F32),
                   jax.ShapeDtypeStruct((nb, CONV_W - 1, GDN_CONV_DIM), F32),
                   jax.ShapeDtypeStruct((nb, GDN_HV, GDN_DK, GDN_DV), F32)],
        scratch_shapes=[pltpu.VMEM((HALO + c, GDN_CONV_DIM), F32),
                        pltpu.VMEM((c, GDN_CONV_DIM), F32),
                        pltpu.VMEM((GDN_HV, GDN_DK, GDN_DV), F32),
                        pltpu.VMEM((LANES, c), F32)],
        compiler_params=pltpu.CompilerParams(
            dimension_semantics=("arbitrary", "arbitrary"), vmem_limit_bytes=VMEM_LIMIT),
        name=f"gdn_c{c}",
    )(pm, ps, pst, conv_in, s_in, conv_w, alog_r, dtb_r, alog_r.reshape(LANES, 1), dtb_r.reshape(LANES, 1),
      g_norm_w.reshape(1, GDN_DV))


def _mlstm_kernel(pm_ref, ps_ref, pst_ref, c_in_ref, n_in_ref, m_in_ref, gb_r_ref, gb_c_ref, nw_ref,
                  out_ref, c_out_ref, n_out_ref, m_out_ref,
                  c_s, n_s, m_s, *, c):
    step = pl.program_id(1)

    @pl.when(step == 0)
    def _():
        c_s[...] = c_in_ref[...]
        n_s[...] = n_in_ref[...]
        m_s[...] = m_in_ref[...]

    row, col = _iota2(c)
    causal = row >= col
    pre_r = _softcap(ps_ref[...] + gb_r_ref[...])
    bcum_all = _dot(jnp.where(causal, 1.0, 0.0).astype(F32), _log_sigmoid(pre_r), HP)
    pre_t = _softcap(pst_ref[...] + gb_c_ref[...])
    bcum_t = _dot(_log_sigmoid(pre_t), jnp.where(row <= col, 1.0, 0.0).astype(F32), HP)

    for h in range(ML_H):
        q = pm_ref[:, h * ML_DQK:(h + 1) * ML_DQK] * (ML_DQK ** -0.5)
        k = pm_ref[:, ML_QK_DIM + h * ML_DQK:ML_QK_DIM + (h + 1) * ML_DQK]
        v = pm_ref[:, 2 * ML_QK_DIM + h * ML_DV:2 * ML_QK_DIM + (h + 1) * ML_DV]
        og = pm_ref[:, 2 * ML_QK_DIM + ML_V_DIM + h * ML_DV:2 * ML_QK_DIM + ML_V_DIM + (h + 1) * ML_DV]
        li_c = pre_r[:, LANE_I + h:LANE_I + h + 1]
        b_c = bcum_all[:, LANE_F + h:LANE_F + h + 1]
        li_r = pre_t[LANE_I + h:LANE_I + h + 1, :]
        b_r = bcum_t[LANE_F + h:LANE_F + h + 1, :]
        m_prev = m_s[:, h:h + 1]
        dlog = jnp.where(causal, b_c - b_r + li_r, -jnp.inf)
        dmax = jnp.max(dlog, axis=-1, keepdims=True)
        inter = b_c + m_prev
        mt = jnp.maximum(inter, dmax)
        qb = q.astype(BF16)
        kb16 = k.astype(BF16)
        vb = v.astype(BF16)
        pw = jnp.exp(dlog - mt) * _dot_nt(qb, kb16)
        sc_in = jnp.exp(inter - mt)
        cs = c_s[h]
        num = sc_in * _dot(qb, cs.astype(BF16)) + _dot(pw.astype(BF16), vb)
        den = sc_in * jnp.sum(q * n_s[h], axis=-1, keepdims=True) + jnp.sum(pw, axis=-1, keepdims=True)
        hh = num / jnp.maximum(jnp.abs(den), jnp.exp(-mt))
        b_last = b_r[:, c - 1:c]
        w_c = b_last - b_c + li_c
        w_r = b_last - b_r + li_r
        m_new = jnp.maximum(b_last + m_prev, jnp.max(w_r, axis=-1, keepdims=True))
        sc = jnp.exp(b_last + m_prev - m_new)
        kw = k * jnp.exp(w_c - m_new)
        c_s[h] = sc * cs + _dot_tn(kw.astype(BF16), vb)
        n_s[h] = sc * n_s[h] + jnp.sum(kw, axis=0, keepdims=True)
        m_s[:, h:h + 1] = m_new
        hn = hh * lax.rsqrt(jnp.mean(hh * hh, axis=-1, keepdims=True) + RMS_EPS) * nw_ref[:, h * ML_DV:(h + 1) * ML_DV]
        out_ref[:, h * ML_DV:(h + 1) * ML_DV] = hn * jax.nn.sigmoid(og)

    @pl.when(step == pl.num_programs(1) - 1)
    def _():
        c_out_ref[...] = c_s[...]
        n_out_ref[...] = n_s[...]
        m_out_ref[...] = m_s[...]


def _mlstm_mixer(pm, ps, pst, c_in, n_in, m_in, gate_bias, norm_w, *, row0, nb, nch, c):
    assert row0 % c == 0
    blk0 = row0 // c
    gb_r = (jnp.zeros((1, LANES), F32).at[0, LANE_I:LANE_I + ML_H].set(gate_bias[0])
            .at[0, LANE_F:LANE_F + ML_H].set(gate_bias[1]))
    rows = lambda b, n: (blk0 + b * nch + n, 0)
    const2 = lambda b, n: (0, 0)
    out, c_out, n_out, m_out = pl.pallas_call(
        functools.partial(_mlstm_kernel, c=c),
        grid=(nb, nch),
        in_specs=[pl.BlockSpec((c, ML_COLS), lambda b, n: (blk0 + b * nch + n, 1)),
                  pl.BlockSpec((c, LANES), rows),
                  pl.BlockSpec((None, None, LANES, c), lambda b, n: (b, n, 0, 0)),
                  pl.BlockSpec((None, ML_H, ML_DQK, ML_DV), lambda b, n: (b, 0, 0, 0)),
                  pl.BlockSpec((None, ML_H, 1, ML_DQK), lambda b, n: (b, 0, 0, 0)),
                  pl.BlockSpec((None, 1, ML_H), lambda b, n: (b, 0, 0)),
                  pl.BlockSpec((1, LANES), const2),
                  pl.BlockSpec((LANES, 1), const2),
                  pl.BlockSpec((1, ML_V_DIM), const2)],
        out_specs=[pl.BlockSpec((c, ML_V_DIM), lambda b, n: (b * nch + n, 0)),
                   pl.BlockSpec((None, ML_H, ML_DQK, ML_DV), lambda b, n: (b, 0, 0, 0)),
                   pl.BlockSpec((None, ML_H, 1, ML_DQK), lambda b, n: (b, 0, 0, 0)),
                   pl.BlockSpec((None, 1, ML_H), lambda b, n: (b, 0, 0))],
        out_shape=[jax.ShapeDtypeStruct((nb * nch * c, ML_V_DIM), F32),
                   jax.ShapeDtypeStruct((nb, ML_H, ML_DQK, ML_DV), F32),
                   jax.ShapeDtypeStruct((nb, ML_H, 1, ML_DQK), F32),
                   jax.ShapeDtypeStruct((nb, 1, ML_H), F32)],
        scratch_shapes=[pltpu.VMEM((ML_H, ML_DQK, ML_DV), F32),
                        pltpu.VMEM((ML_H, 1, ML_DQK), F32),
                        pltpu.VMEM((1, ML_H), F32)],
        compiler_params=pltpu.CompilerParams(
            dimension_semantics=("arbitrary", "arbitrary"), vmem_limit_bytes=VMEM_LIMIT),
        name=f"mlstm_c{c}",
    )(pm, ps, pst, c_in, n_in.reshape(nb, ML_H, 1, ML_DQK), m_in.reshape(nb, 1, ML_H),
      gb_r, gb_r.reshape(LANES, 1), norm_w.reshape(1, ML_V_DIM))
    return out, c_out, n_out.reshape(nb, ML_H, ML_DQK), m_out.reshape(nb, ML_H)


def _mix_rows(pm, ps, state, p, *, row0, nb, nch, c):
    conv_w, a_log, dt_bias, g_norm_w, ml_gb, ml_nw = p
    conv_in, s_in, c_in, n_in, m_in = state
    n_rows = nb * nch * c
    pst = jnp.swapaxes(ps[row0:row0 + n_rows].reshape(nb, nch, c, LANES), 2, 3)
    og, conv_out, s_out = _gdn_mixer(pm, ps, pst, conv_in, s_in, conv_w, a_log, dt_bias, g_norm_w,
                                     row0=row0, nb=nb, nch=nch, c=c)
    om, c_out, n_out, m_out = _mlstm_mixer(pm, ps, pst, c_in, n_in, m_in, ml_gb, ml_nw,
                                           row0=row0, nb=nb, nch=nch, c=c)
    return jnp.concatenate([og, om], axis=1), (conv_out, s_out, c_out, n_out, m_out)


def kernel(x_prompt, x_sample, state_gdn_conv, state_gdn, state_mlstm_c, state_mlstm_n, state_mlstm_m,
           meta_tokens, norm_mix_w, w_in, gdn_conv_w, gdn_a_log, gdn_dt_bias, gdn_norm_w,
           ml_gate_bias, ml_norm_w, w_out, norm_ffn_w, router_w, router_b,
           w_gate, b_gate, w_up, b_up, w_down, b_down, norm_final_w):
    B, S, D = x_prompt.shape
    BS, TS, _ = x_sample.shape
    n_main = B * S
    n_meta = B * N_META
    row_meta = n_main
    row_samp = n_main + n_meta
    n_tok = row_samp + BS * TS
    meta = jnp.broadcast_to(meta_tokens.astype(x_prompt.dtype), (B, N_META, D))
    x_all = jnp.concatenate([x_prompt.reshape(n_main, D), meta.reshape(n_meta, D),
                             x_sample.reshape(BS * TS, D)], axis=0)

    w = w_in[0]
    offs = [0]
    for s in IN_SPLITS:
        offs.append(offs[-1] + s)
    col = lambda i: w[:, offs[i]:offs[i + 1]]
    w_main = jnp.concatenate([col(0), col(1), col(2), col(3), col(6), col(7), col(8), col(9)], axis=1)
    w_small = jnp.concatenate([col(4), col(5), col(10), col(11)], axis=1)
    w_small = jnp.pad(w_small, ((0, 0), (0, LANES - w_small.shape[1])))

    tm = n_tok // 10
    h = _rmsnorm(x_all, norm_mix_w[0], BF16, n_tok // 20)
    proj_main = _matmul(h, w_main, tm, 512)
    proj_small = _matmul(h, w_small, tm, LANES)

    mix_p = (gdn_conv_w[0], gdn_a_log[0], gdn_dt_bias[0], gdn_norm_w[0], ml_gate_bias[0], ml_norm_w[0])
    p_init = (jnp.zeros((B, CONV_W - 1, GDN_CONV_DIM), F32),
              jnp.zeros((B, GDN_HV, GDN_DK, GDN_DV), F32),
              jnp.zeros((B, ML_H, ML_DQK, ML_DV), F32),
              jnp.zeros((B, ML_H, ML_DQK), F32),
              jnp.zeros((B, ML_H), F32))
    mixed_meta, st = _mix_rows(proj_main, proj_small, p_init, mix_p, row0=row_meta, nb=B, nch=1, c=N_META)
    c_main = math.gcd(S, GDN_CHUNK)
    mixed_main, p_st = _mix_rows(proj_main, proj_small, st, mix_p, row0=0, nb=B, nch=S // c_main, c=c_main)
    s_init = (state_gdn_conv[0], state_gdn[0], state_mlstm_c[0], state_mlstm_n[0], state_mlstm_m[0])
    c_samp = math.gcd(TS, GDN_CHUNK)
    mixed_samp, s_st = _mix_rows(proj_main, proj_small, s_init, mix_p, row0=row_samp, nb=BS,
                                 nch=TS // c_samp, c=c_samp)
    mixed = jnp.concatenate([mixed_main, mixed_meta, mixed_samp], axis=0).astype(BF16)

    x1 = _matmul(mixed, w_out[0], tm, 512, residual=x_all)

    h2, gates_l, idx_l = _router(x1, norm_ffn_w[0], router_w[0], router_b[0], n_tok // 20)
    rank_l, cnt = _expert_ranks(idx_l, n_tok // 20)
    counts = cnt[0, :N_EXPERTS]
    padded = (counts + MOE_TM - 1) // MOE_TM * MOE_TM
    pad_end = jnp.cumsum(padded)
    pad_start = pad_end - padded
    n_assign = n_tok * TOP_K
    dest = (pad_start[idx_l[:, :TOP_K]] + rank_l[:, :TOP_K]).reshape(n_assign)
    n_blocks = -(-n_assign // MOE_TM) + N_EXPERTS
    n_rows = n_blocks * MOE_TM
    block_lo = jnp.arange(n_blocks, dtype=jnp.int32) * MOE_TM
    block_e = jnp.minimum(jnp.sum((pad_end[None, :] <= block_lo[:, None]).astype(jnp.int32), axis=1),
                          N_EXPERTS - 1)
    n_valid = pad_end[-1] // MOE_TM
    row_tok = jnp.zeros((n_rows,), jnp.int32).at[dest].set(jnp.arange(n_assign, dtype=jnp.int32) // TOP_K)
    xs = h2[row_tok]
    y = _moe_experts(xs, block_e, n_valid, w_gate[0], b_gate[0], w_up[0], b_up[0],
                     w_down[0], b_down[0], 512, 1024)
    y_all = _combine_norm(x1, gates_l, dest, y, norm_final_w, COMBINE_TB)
    y_prompt = y_all[:n_main].reshape(B, S, D)
    y_sample = y_all[row_samp:].reshape(BS, TS, D)

    likes = (state_gdn_conv, state_gdn, state_mlstm_c, state_mlstm_n, state_mlstm_m)
    p_out = tuple(a[None].astype(l.dtype) for a, l in zip(p_st, likes))
    s_out = tuple(a[None].astype(l.dtype) for a, l in zip(s_st, likes))
    return (y_prompt, y_sample) + p_out + s_out
```

```python
import functools
import math

import jax
import jax.numpy as jnp
from jax import lax
from jax.experimental import pallas as pl
from jax.experimental.pallas import tpu as pltpu

F32 = jnp.float32
BF16 = jnp.bfloat16
U32 = jnp.uint32
HP = lax.Precision.HIGHEST

D_MODEL = 4096
N_META = 16
RMS_EPS = 1e-6
GDN_DK = 128
GDN_DV = 128
GDN_HV = 16
GDN_HK = 8
GDN_KEY_DIM = GDN_HK * GDN_DK
GDN_VAL_DIM = GDN_HV * GDN_DV
GDN_CONV_DIM = 2 * GDN_KEY_DIM + GDN_VAL_DIM
CONV_W = 4
GDN_CHUNK = 64
ML_DQK = 256
ML_DV = 512
ML_H = 4
ML_QK_DIM = ML_H * ML_DQK
ML_V_DIM = ML_H * ML_DV
ML_CHUNK = 64
GATE_SOFTCAP = 15.0
IN_SPLITS = (GDN_KEY_DIM, GDN_KEY_DIM, GDN_VAL_DIM, GDN_VAL_DIM, GDN_HV, GDN_HV,
             ML_QK_DIM, ML_QK_DIM, ML_V_DIM, ML_V_DIM, ML_H, ML_H)
N_EXPERTS = 32
TOP_K = 4
SWIGLU_LIMIT = 7.0
SWIGLU_ALPHA = 1.702

LANES = 128
VMEM_LIMIT = 56 * 1024 * 1024
MOE_TM = 512
MOE_SUB = 256
NEG_BIG = -1e30
COMBINE_TB = 32
DISPATCH_TB = 160
HALO = 8
GDN_COLS = GDN_CONV_DIM + GDN_VAL_DIM
ML_COLS = 2 * ML_QK_DIM + 2 * ML_V_DIM
LANE_DT = GDN_HV
LANE_I = 2 * GDN_HV
LANE_F = 2 * GDN_HV + ML_H


def _dot(a, b, precision=None):
    return jnp.dot(a, b, preferred_element_type=F32, precision=precision)


def _split_bf16(a):
    hi = a.astype(BF16)
    return hi, (a - hi.astype(F32)).astype(BF16)


def _dot3(a, b):
    ah, al = _split_bf16(a)
    bh, bl = _split_bf16(b)
    return _dot(ah, bh) + _dot(al, bh) + _dot(ah, bl)


HIGH_HALF = 0xFFFF0000


def _pack_halves(h):
    half = h.shape[1] // 2
    bits = lax.bitcast_convert_type(h.astype(BF16).astype(F32), U32)
    return (bits[:, :half] >> 16) | (bits[:, half:] & jnp.uint32(HIGH_HALF))


def _unpack_halves(x):
    lo = lax.bitcast_convert_type(x << 16, F32).astype(BF16)
    hi = lax.bitcast_convert_type(x & jnp.uint32(HIGH_HALF), F32).astype(BF16)
    return lo, hi


def _dot_nt(a, b):
    return lax.dot_general(a, b, (((1,), (1,)), ((), ())), preferred_element_type=F32)


def _dot_tn(a, b):
    return lax.dot_general(a, b, (((0,), (0,)), ((), ())), preferred_element_type=F32)


def _rmsnorm_kernel(x_ref, w_ref, o_ref):
    x = x_ref[...]
    y = x * lax.rsqrt(jnp.mean(x * x, axis=-1, keepdims=True) + RMS_EPS)
    o_ref[...] = (y * w_ref[...]).astype(o_ref.dtype)


def _rmsnorm(x, w, out_dtype, tr):
    n, d = x.shape
    return pl.pallas_call(
        _rmsnorm_kernel,
        grid=(n // tr,),
        in_specs=[pl.BlockSpec((tr, d), lambda i: (i, 0)),
                  pl.BlockSpec((1, d), lambda i: (0, 0))],
        out_specs=pl.BlockSpec((tr, d), lambda i: (i, 0)),
        out_shape=jax.ShapeDtypeStruct((n, d), out_dtype),
        compiler_params=pltpu.CompilerParams(
            dimension_semantics=("arbitrary",), vmem_limit_bytes=VMEM_LIMIT),
        name="rmsnorm",
    )(x, w.reshape(1, d))


def _mm_kernel(x_ref, w_ref, o_ref, wb_ref):
    @pl.when(pl.program_id(1) == 0)
    def _():
        wb_ref[...] = w_ref[...].astype(BF16)

    o_ref[...] = _dot(x_ref[...], wb_ref[...])


def _mm_res_kernel(x_ref, w_ref, r_ref, o_ref, wb_ref):
    @pl.when(pl.program_id(1) == 0)
    def _():
        wb_ref[...] = w_ref[...].astype(BF16)

    o_ref[...] = r_ref[...] + _dot(x_ref[...], wb_ref[...])


def _matmul(x, w, tm, tn, residual=None):
    m, k = x.shape
    n = w.shape[1]
    in_specs = [pl.BlockSpec((tm, k), lambda j, i: (i, 0)),
                pl.BlockSpec((k, tn), lambda j, i: (0, j))]
    args = [x, w]
    body = _mm_kernel
    if residual is not None:
        in_specs.append(pl.BlockSpec((tm, tn), lambda j, i: (i, j)))
        args.append(residual)
        body = _mm_res_kernel
    return pl.pallas_call(
        body,
        grid=(n // tn, m // tm),
        in_specs=in_specs,
        out_specs=pl.BlockSpec((tm, tn), lambda j, i: (i, j)),
        out_shape=jax.ShapeDtypeStruct((m, n), F32),
        scratch_shapes=[pltpu.VMEM((k, tn), BF16)],
        compiler_params=pltpu.CompilerParams(
            dimension_semantics=("arbitrary", "arbitrary"), vmem_limit_bytes=VMEM_LIMIT),
        name="matmul",
    )(*args)


def _router_kernel(x_ref, nw_ref, rw_ref, rb_ref, h_ref, gate_ref, idx_ref):
    x = x_ref[...]
    h = x * lax.rsqrt(jnp.mean(x * x, axis=-1, keepdims=True) + RMS_EPS) * nw_ref[...]
    h_ref[...] = _pack_halves(h)
    logits =_dot(h, rw_ref[...], HP) + rb_ref[...]
    lane = lax.broadcasted_iota(jnp.int32, logits.shape, 1)
    vals, idxs = [], []
    cur = logits
    for _ in range(TOP_K):
        mx = jnp.max(cur, axis=-1, keepdims=True)
        ix = jnp.min(jnp.where(cur == mx, lane, LANES), axis=-1, keepdims=True)
        vals.append(mx)
        idxs.append(ix)
        cur = jnp.where(lane == ix, NEG_BIG, cur)
    exps = [jnp.exp(v - vals[0]) for v in vals]
    denom = exps[0] + exps[1] + exps[2] + exps[3]
    gate_out = jnp.zeros(logits.shape, F32)
    idx_out = jnp.zeros(logits.shape, jnp.int32)
    for k in range(TOP_K):
        gate_out = jnp.where(lane == k, exps[k] / denom, gate_out)
        idx_out = jnp.where(lane == k, idxs[k], idx_out)
    gate_ref[...] = gate_out
    idx_ref[...] = idx_out


def _router(x, norm_w, router_w, router_b, tr):
    n, d = x.shape
    rw = jnp.zeros((d, LANES), F32).at[:, :N_EXPERTS].set(router_w)
    rb = jnp.full((1, LANES), NEG_BIG, F32).at[0, :N_EXPERTS].set(router_b)
    h, gates, idx = pl.pallas_call(
        _router_kernel,
        grid=(n // tr,),
        in_specs=[pl.BlockSpec((tr, d), lambda i: (i, 0)),
                  pl.BlockSpec((1, d), lambda i: (0, 0)),
                  pl.BlockSpec((d, LANES), lambda i: (0, 0)),
                  pl.BlockSpec((1, LANES), lambda i: (0, 0))],
        out_specs=[pl.BlockSpec((tr, d // 2), lambda i: (i, 0)),
                   pl.BlockSpec((tr, LANES), lambda i: (i, 0)),
                   pl.BlockSpec((tr, LANES), lambda i: (i, 0))],
        out_shape=[jax.ShapeDtypeStruct((n, d // 2), U32),
                   jax.ShapeDtypeStruct((n, LANES), F32),
                   jax.ShapeDtypeStruct((n, LANES), jnp.int32)],
        compiler_params=pltpu.CompilerParams(
            dimension_semantics=("arbitrary",), vmem_limit_bytes=VMEM_LIMIT),
        name="router",
    )(x, norm_w.reshape(1, d), rw, rb)
    return h, gates, idx


T_EXPERT, T_FIRST, T_GROUP, T_NEXT_EXPERT, T_LAST_GROUP, T_ROWS, T_COLS = range(7)


def _moe_block_tables(block_e, n_valid, block_rows):
    nb = block_e.shape[0]
    valid = jnp.arange(nb) < n_valid
    prev = jnp.concatenate([block_e[:1] - 1, block_e[:-1]])
    first = jnp.logical_and(valid, block_e != prev)
    group = jnp.cumsum(first.astype(jnp.int32)) - 1
    n_groups = jnp.sum(first.astype(jnp.int32))
    group_e = jnp.zeros((nb,), jnp.int32).at[jnp.where(first, group, nb)].set(block_e, mode="drop")
    last = group == n_groups - 1
    nxt = jnp.where(last, block_e[0], group_e[jnp.minimum(group + 1, nb - 1)])
    tab = jnp.stack([block_e, first.astype(jnp.int32), group, nxt, last.astype(jnp.int32), block_rows], axis=1)
    meta = jnp.stack([n_valid.astype(jnp.int32), n_groups.astype(jnp.int32)])
    return tab.astype(jnp.int32).reshape(-1), meta


def _tab(tab_ref, b, col):
    return tab_ref[b * T_COLS + col]


def _weight_copy(w_hbm, buf, sem, e, j, slot, tn):
    return pltpu.make_async_copy(w_hbm.at[e, :, pl.ds(pl.multiple_of(j * tn, tn), tn)], buf.at[slot], sem.at[slot])


def _prefetch_step(tab_ref, meta_ref, weights, tn):
    j = pl.program_id(0)
    b = pl.program_id(1)
    nj = pl.num_programs(0)

    @pl.when(_tab(tab_ref, b, T_FIRST) == 1)
    def _():
        e = _tab(tab_ref, b, T_EXPERT)
        slot = (j * meta_ref[1] + _tab(tab_ref, b, T_GROUP)) % 2

        @pl.when(jnp.logical_and(j == 0, b == 0))
        def _():
            for w_hbm, buf, sem, _ in weights:
                _weight_copy(w_hbm, buf, sem, e, j, slot, tn).start()

        for w_hbm, buf, sem, _ in weights:
            _weight_copy(w_hbm, buf, sem, e, j, slot, tn).wait()

        last = _tab(tab_ref, b, T_LAST_GROUP) == 1
        j_next = jnp.where(last, j + 1, j)

        @pl.when(j_next < nj)
        def _():
            for w_hbm, buf, sem, _ in weights:
                _weight_copy(w_hbm, buf, sem, _tab(tab_ref, b, T_NEXT_EXPERT), j_next, 1 - slot, tn).start()

        for _, buf, _, wb in weights:
            wb[...] = buf[slot].astype(BF16)


def _moe_up_kernel(tab_ref, meta_ref, x_ref, wg_hbm, wu_hbm, bg_ref, bu_ref, h_ref,
                   wg_buf, wu_buf, wg_sem, wu_sem, wgb, wub, *, tn):
    b = pl.program_id(1)
    _prefetch_step(tab_ref, meta_ref, [(wg_hbm, wg_buf, wg_sem, wgb), (wu_hbm, wu_buf, wu_sem, wub)], tn)

    for s in range(MOE_TM // MOE_SUB):
        rows = pl.ds(s * MOE_SUB, MOE_SUB)
        used = jnp.logical_and(b < meta_ref[0], _tab(tab_ref, b, T_ROWS) > s * MOE_SUB)

        @pl.when(used)
        def _():
            xa, xb = _unpack_halves(x_ref[rows, :])
            half = xa.shape[1]
            g = _dot(xa, wgb[:half, :]) + _dot(xb, wgb[half:, :]) + bg_ref[...]
            u = _dot(xa, wub[:half, :]) + _dot(xb, wub[half:, :]) + bu_ref[...]
            gl = jnp.minimum(g, SWIGLU_LIMIT)
            lin = jnp.clip(u, -SWIGLU_LIMIT, SWIGLU_LIMIT)
            h_ref[rows, :] = (gl * jax.nn.sigmoid(SWIGLU_ALPHA * gl) * (lin + 1.0)).astype(h_ref.dtype)

        @pl.when(jnp.logical_not(used))
        def _():
            h_ref[rows, :] = jnp.zeros((MOE_SUB, h_ref.shape[1]), h_ref.dtype)


def _moe_down_kernel(tab_ref, meta_ref, h_ref, wd_hbm, bd_ref, y_ref,
                     wd_buf, wd_sem, wdb, *, tn):
    b = pl.program_id(1)
    _prefetch_step(tab_ref, meta_ref, [(wd_hbm, wd_buf, wd_sem, wdb)], tn)

    for s in range(MOE_TM // MOE_SUB):
        rows = pl.ds(s * MOE_SUB, MOE_SUB)
        used = jnp.logical_and(b < meta_ref[0], _tab(tab_ref, b, T_ROWS) > s * MOE_SUB)

        @pl.when(used)
        def _():
            y_ref[rows, :] = _dot(h_ref[rows, :], wdb[...]) + bd_ref[...]

        @pl.when(jnp.logical_not(used))
        def _():
            y_ref[rows, :] = jnp.zeros((MOE_SUB, y_ref.shape[1]), y_ref.dtype)


def _moe_experts(xs, block_e, n_valid, block_rows, w_gate, b_gate, w_up, b_up, w_down, b_down, tn_up, tn_down):
    r = xs.shape[0]
    d = 2 * xs.shape[1]
    nb = r // MOE_TM
    ne, _, dff = w_gate.shape
    tab, meta = _moe_block_tables(block_e, n_valid, block_rows)
    cp = pltpu.CompilerParams(dimension_semantics=("arbitrary", "arbitrary"), vmem_limit_bytes=VMEM_LIMIT)
    e_of = lambda j, b, tab, meta: (tab[b * T_COLS + T_EXPERT], 0, j)
    h = pl.pallas_call(
        functools.partial(_moe_up_kernel, tn=tn_up),
        grid_spec=pltpu.PrefetchScalarGridSpec(
            num_scalar_prefetch=2,
            grid=(dff // tn_up, nb),
            in_specs=[pl.BlockSpec((MOE_TM, d // 2), lambda j, b, tab, meta: (b, 0)),
                      pl.BlockSpec(memory_space=pl.ANY),
                      pl.BlockSpec(memory_space=pl.ANY),
                      pl.BlockSpec((None, 1, tn_up), e_of),
                      pl.BlockSpec((None, 1, tn_up), e_of)],
            out_specs=pl.BlockSpec((MOE_TM, tn_up), lambda j, b, tab, meta: (b, j)),
            scratch_shapes=[pltpu.VMEM((2, d, tn_up), F32), pltpu.VMEM((2, d, tn_up), F32),
                            pltpu.SemaphoreType.DMA((2,)), pltpu.SemaphoreType.DMA((2,)),
                            pltpu.VMEM((d, tn_up), BF16), pltpu.VMEM((d, tn_up), BF16)]),
        out_shape=jax.ShapeDtypeStruct((r, dff), BF16),
        compiler_params=cp,
        name="moe_up",
    )(tab, meta, xs, w_gate, w_up, b_gate.reshape(ne, 1, dff), b_up.reshape(ne, 1, dff))
    return pl.pallas_call(
        functools.partial(_moe_down_kernel, tn=tn_down),
        grid_spec=pltpu.PrefetchScalarGridSpec(
            num_scalar_prefetch=2,
            grid=(d // tn_down, nb),
            in_specs=[pl.BlockSpec((MOE_TM, dff), lambda j, b, tab, meta: (b, 0)),
                      pl.BlockSpec(memory_space=pl.ANY),
                      pl.BlockSpec((None, 1, tn_down), e_of)],
            out_specs=pl.BlockSpec((MOE_TM, tn_down), lambda j, b, tab, meta: (b, j)),
            scratch_shapes=[pltpu.VMEM((2, dff, tn_down), F32), pltpu.SemaphoreType.DMA((2,)),
                            pltpu.VMEM((dff, tn_down), BF16)]),
        out_shape=jax.ShapeDtypeStruct((r, d), F32),
        compiler_params=cp,
        name="moe_down",
    )(tab, meta, h, w_down, b_down.reshape(ne, 1, d))


def _rank_kernel(idx_ref, rank_ref, cnt_ref, carry):
    @pl.when(pl.program_id(0) == 0)
    def _():
        carry[...] = jnp.zeros(carry.shape, F32)

    idx = idx_ref[...]
    tb = idx.shape[0]
    lane = lax.broadcasted_iota(jnp.int32, idx.shape, 1)
    hits = [lane == idx[:, k:k + 1] for k in range(TOP_K)]
    picked = sum(jnp.where(hit, 1.0, 0.0) for hit in hits)
    row = lax.broadcasted_iota(jnp.int32, (tb, tb), 0)
    col = lax.broadcasted_iota(jnp.int32, (tb, tb), 1)
    before = jnp.where(row > col, 1.0, 0.0).astype(BF16)
    seen = _dot(before, picked.astype(BF16)) + carry[...]
    rank = jnp.zeros(idx.shape, F32)
    for k in range(TOP_K):
        rk = jnp.sum(jnp.where(hits[k], seen, 0.0), axis=-1, keepdims=True)
        rank = jnp.where(lane == k, rk, rank)
    rank_ref[...] = rank.astype(jnp.int32)
    carry[...] = carry[...] + jnp.sum(picked, axis=0, keepdims=True)
    cnt_ref[...] = carry[...].astype(jnp.int32)


def _expert_ranks(top_idx_lanes, tb):
    n = top_idx_lanes.shape[0]
    return pl.pallas_call(
        _rank_kernel,
        grid=(n // tb,),
        in_specs=[pl.BlockSpec((tb, LANES), lambda i: (i, 0))],
        out_specs=[pl.BlockSpec((tb, LANES), lambda i: (i, 0)),
                   pl.BlockSpec((1, LANES), lambda i: (0, 0))],
        out_shape=[jax.ShapeDtypeStruct((n, LANES), jnp.int32),
                   jax.ShapeDtypeStruct((1, LANES), jnp.int32)],
        scratch_shapes=[pltpu.VMEM((1, LANES), F32)],
        compiler_params=pltpu.CompilerParams(dimension_semantics=("arbitrary",)),
        name="expert_ranks",
    )(top_idx_lanes)


def _dispatch_copy(h_ref, xs_hbm, sem, r, row):
    return pltpu.make_async_copy(h_ref.at[pl.ds(r, 1), :], xs_hbm.at[pl.ds(row, 1), :], sem.at[0])


def _dispatch_kernel(dest_ref, h_ref, xs_init_hbm, xs_hbm, sem, *, tb):
    del xs_init_hbm
    i = pl.program_id(0)

    def start(r, carry):
        base = (i * tb + r) * TOP_K
        for k in range(TOP_K):
            _dispatch_copy(h_ref, xs_hbm, sem, r, dest_ref[base + k]).start()
        return carry

    def wait(r, carry):
        for k in range(TOP_K):
            _dispatch_copy(h_ref, xs_hbm, sem, r, 0).wait()
        return carry

    lax.fori_loop(0, tb, start, 0)
    lax.fori_loop(0, tb, wait, 0)


def _dispatch_rows(h, dest, n_rows, tb):
    n, w = h.shape
    return pl.pallas_call(
        functools.partial(_dispatch_kernel, tb=tb),
        grid_spec=pltpu.PrefetchScalarGridSpec(
            num_scalar_prefetch=1,
            grid=(n // tb,),
            in_specs=[pl.BlockSpec((tb, w), lambda i, dest: (i, 0)),
                      pl.BlockSpec(memory_space=pl.ANY)],
            out_specs=pl.BlockSpec(memory_space=pl.ANY),
            scratch_shapes=[pltpu.SemaphoreType.DMA((1,))]),
        out_shape=jax.ShapeDtypeStruct((n_rows, w), h.dtype),
        input_output_aliases={2: 0},
        compiler_params=pltpu.CompilerParams(dimension_semantics=("arbitrary",), vmem_limit_bytes=VMEM_LIMIT),
        name="dispatch_rows",
    )(dest, h, jnp.zeros((n_rows, w), h.dtype))


def _row_copy(y_hbm, ybuf, sem, src_row, slot, k, r):
    return pltpu.make_async_copy(y_hbm.at[pl.ds(src_row, 1), :], ybuf.at[slot, k, pl.ds(r, 1), :], sem.at[slot])


def _combine_kernel(dest_ref, x_ref, gate_ref, w_ref, y_hbm, o_ref, ybuf, sem, *, tb):
    i = pl.program_id(0)
    n = pl.num_programs(0)

    def start_block(blk, slot):
        def body(r, carry):
            base = (blk * tb + r) * TOP_K
            for k in range(TOP_K):
                _row_copy(y_hbm, ybuf, sem, dest_ref[base + k], slot, k, r).start()
            return carry
        lax.fori_loop(0, tb, body, 0)

    def wait_block(slot):
        def body(r, carry):
            for k in range(TOP_K):
                _row_copy(y_hbm, ybuf, sem, 0, slot, k, r).wait()
            return carry
        lax.fori_loop(0, tb, body, 0)

    slot = i % 2

    @pl.when(i == 0)
    def _():
        start_block(0, 0)

    @pl.when(i + 1 < n)
    def _():
        start_block(i + 1, 1 - slot)

    wait_block(slot)
    gate = gate_ref[...]
    acc = x_ref[...]
    for k in range(TOP_K):
        acc = acc + gate[:, k:k + 1] * ybuf[slot, k]
    o_ref[...] = acc * lax.rsqrt(jnp.mean(acc * acc, axis=-1, keepdims=True) + RMS_EPS) * w_ref[...]


def _combine_norm(x, gates_lanes, dest, y, norm_w, tb):
    n, d = x.shape
    return pl.pallas_call(
        functools.partial(_combine_kernel, tb=tb),
        grid_spec=pltpu.PrefetchScalarGridSpec(
            num_scalar_prefetch=1,
            grid=(n // tb,),
            in_specs=[pl.BlockSpec((tb, d), lambda i, dest: (i, 0)),
                      pl.BlockSpec((tb, LANES), lambda i, dest: (i, 0)),
                      pl.BlockSpec((1, d), lambda i, dest: (0, 0)),
                      pl.BlockSpec(memory_space=pl.ANY)],
            out_specs=pl.BlockSpec((tb, d), lambda i, dest: (i, 0)),
            scratch_shapes=[pltpu.VMEM((2, TOP_K, tb, d), F32), pltpu.SemaphoreType.DMA((2,))]),
        out_shape=jax.ShapeDtypeStruct((n, d), F32),
        compiler_params=pltpu.CompilerParams(dimension_semantics=("arbitrary",), vmem_limit_bytes=VMEM_LIMIT),
        name="combine_norm",
    )(dest, x, gates_lanes, norm_w.reshape(1, d), y)


def _softplus(x):
    return jnp.maximum(x, 0.0) + jnp.log1p(jnp.exp(-jnp.abs(x)))


def _log_sigmoid(x):
    return -_softplus(-x)


def _softcap(x):
    return GATE_SOFTCAP * jnp.tanh(x / GATE_SOFTCAP)


def _iota2(c):
    return (lax.broadcasted_iota(jnp.int32, (c, c), 0), lax.broadcasted_iota(jnp.int32, (c, c), 1))


def _unit_lower_inverse(a, c):
    row, col = _iota2(c)
    eye = jnp.where(row == col, 1.0, 0.0).astype(F32)
    blk = min(c, 16)
    sh = int(math.log2(blk))
    same = jnp.right_shift(row, sh) == jnp.right_shift(col, sh)
    n = jnp.where(same, -a, 0.0)
    t = eye + n
    p = n
    size = 2
    while size < blk:
        p = _dot3(p, p)
        yield
        t = t + _dot3(t, p)
        yield
        size *= 2
    m = blk
    while m < c:
        s1 = int(math.log2(m))
        in_m = jnp.right_shift(row, s1) == jnp.right_shift(col, s1)
        in_2m = jnp.right_shift(row, s1 + 1) == jnp.right_shift(col, s1 + 1)
        e = jnp.where(in_2m, jnp.where(in_m, 0.0, a), 0.0)
        et = _dot3(e, t)
        yield
        t = t - _dot3(t, et)
        yield
        m *= 2
    return t


def _interleave(gens):
    results = [None] * len(gens)
    live = list(range(len(gens)))
    while live:
        nxt = []
        for i in live:
            try:
                next(gens[i])
                nxt.append(i)
            except StopIteration as stop:
                results[i] = stop.value
        live = nxt
    return results


def _gdn_kernel(pm_ref, ps_ref, pst_ref, conv_in_ref, s_in_ref, convw_ref, alog_r_ref, dtb_r_ref,
                alog_c_ref, dtb_c_ref, gnw_ref,
                out_ref, conv_out_ref, s_out_ref,
                xbuf, qkv_s, s_s, gct_s, *, c):
    n = pl.program_id(1)

    @pl.when(n == 0)
    def _():
        xbuf[HALO - (CONV_W - 1):HALO, :] = conv_in_ref[...]
        s_s[...] = s_in_ref[...]

    xbuf[HALO:HALO + c, :] = pm_ref[:, :GDN_CONV_DIM]
    y = jnp.zeros((c, GDN_CONV_DIM), F32)
    for j in range(CONV_W):
        lo = HALO - (CONV_W - 1) + j
        y = y + xbuf[lo:lo + c, :] * convw_ref[j:j + 1, :]
    xbuf[HALO - (CONV_W - 1):HALO, :] = xbuf[HALO + c - (CONV_W - 1):HALO + c, :]
    y = y * jax.nn.sigmoid(y)
    for hk in range(GDN_HK):
        q = y[:, hk * GDN_DK:(hk + 1) * GDN_DK]
        k = y[:, GDN_KEY_DIM + hk * GDN_DK:GDN_KEY_DIM + (hk + 1) * GDN_DK]
        qkv_s[:, hk * GDN_DK:(hk + 1) * GDN_DK] = (
            q * lax.rsqrt(jnp.sum(q * q, axis=-1, keepdims=True) + 1e-6) * (GDN_DK ** -0.5))
        qkv_s[:, GDN_KEY_DIM + hk * GDN_DK:GDN_KEY_DIM + (hk + 1) * GDN_DK] = (
            k * lax.rsqrt(jnp.sum(k * k, axis=-1, keepdims=True) + 1e-6))
    qkv_s[:, 2 * GDN_KEY_DIM:] = y[:, 2 * GDN_KEY_DIM:]

    ps = ps_ref[...]
    beta_all = jax.nn.sigmoid(ps)
    g_all = -jnp.exp(alog_r_ref[...]) * _softplus(ps + dtb_r_ref[...])
    row, col = _iota2(c)
    causal = row >= col
    strict = row > col
    gc_all = _dot(jnp.where(causal, 1.0, 0.0).astype(F32), g_all, HP)
    g_t = -jnp.exp(alog_c_ref[...]) * _softplus(pst_ref[...] + dtb_c_ref[...])
    gct_s[...] = _dot(g_t, jnp.where(row <= col, 1.0, 0.0).astype(F32), HP)
    lane = lax.broadcasted_iota(jnp.int32, (c, LANES), 1)
    gnw = gnw_ref[...]

    def load_head(h):
        hk = h // (GDN_HV // GDN_HK)
        q = qkv_s[:, pl.ds(pl.multiple_of(hk * GDN_DK, GDN_DK), GDN_DK)]
        k = qkv_s[:, pl.ds(pl.multiple_of(GDN_KEY_DIM + hk * GDN_DK, GDN_DK), GDN_DK)]
        v = qkv_s[:, pl.ds(pl.multiple_of(2 * GDN_KEY_DIM + h * GDN_DV, GDN_DV), GDN_DV)]
        gr = gct_s[pl.ds(LANE_DT + h, 1), :]
        z = pm_ref[:, pl.ds(pl.multiple_of(GDN_CONV_DIM + h * GDN_DV, GDN_DV), GDN_DV)]
        return q, k, v, gr, z, s_s[h]

    def compute_head(h, q, k, v, gr, z, s):
        beta = jnp.sum(jnp.where(lane == h, beta_all, 0.0), axis=-1, keepdims=True)
        gc = jnp.sum(jnp.where(lane == LANE_DT + h, gc_all, 0.0), axis=-1, keepdims=True)
        decay = jnp.where(causal, jnp.exp(jnp.where(causal, gc - gr, 0.0)), 0.0)
        kb = k * beta
        kbf = k.astype(BF16)
        a = jnp.where(strict, _dot_nt(kb.astype(BF16), kbf) * decay, 0.0)
        qk = jnp.where(causal, _dot_nt(q.astype(BF16), kbf) * decay, 0.0)
        yield
        t = yield from _unit_lower_inverse(a, c)
        eg = jnp.exp(gc)
        u = _dot3(t, v * beta)
        w = _dot3(t, kb * eg)
        yield
        sb = s.astype(BF16)
        v_new = u - _dot(w.astype(BF16), sb)
        o = _dot((q * eg).astype(BF16), sb)
        yield
        vnb = v_new.astype(BF16)
        o = o + _dot(qk.astype(BF16), vnb)
        g_last = gr[:, c - 1:c]
        k_dec = k * jnp.exp(g_last - gc)
        s_new = s * jnp.exp(g_last) + _dot_tn(k_dec.astype(BF16), vnb)
        yield
        o = o * lax.rsqrt(jnp.mean(o * o, axis=-1, keepdims=True) + RMS_EPS) * gnw
        return o * (z * jax.nn.sigmoid(z)), s_new

    group = GDN_HV // 2 if c >= GDN_CHUNK else GDN_HV

    def head_group(i, carry):
        hs = [i * group + r for r in range(group)]
        loaded = [load_head(h) for h in hs]
        done = _interleave([compute_head(h, *x) for h, x in zip(hs, loaded)])
        for h, (o, s_new) in zip(hs, done):
            s_s[h] = s_new
            out_ref[:, pl.ds(pl.multiple_of(h * GDN_DV, GDN_DV), GDN_DV)] = o
        return carry

    lax.fori_loop(0, GDN_HV // group, head_group, 0)

    @pl.when(n == pl.num_programs(1) - 1)
    def _():
        conv_out_ref[...] = xbuf[HALO - (CONV_W - 1):HALO, :]
        s_out_ref[...] = s_s[...]


def _gdn_mixer(pm, ps, pst, conv_in, s_in, conv_w, a_log, dt_bias, g_norm_w, *, row0, nb, nch, c):
    assert row0 % c == 0
    blk0 = row0 // c
    alog_r = jnp.zeros((1, LANES), F32).at[0, LANE_DT:LANE_DT + GDN_HV].set(a_log)
    dtb_r = jnp.zeros((1, LANES), F32).at[0, LANE_DT:LANE_DT + GDN_HV].set(dt_bias)
    rows = lambda b, n: (blk0 + b * nch + n, 0)
    const2 = lambda b, n: (0, 0)
    return pl.pallas_call(
        functools.partial(_gdn_kernel, c=c),
        grid=(nb, nch),
        in_specs=[pl.BlockSpec((c, GDN_COLS), rows),
                  pl.BlockSpec((c, LANES), rows),
                  pl.BlockSpec((None, None, LANES, c), lambda b, n: (b, n, 0, 0)),
                  pl.BlockSpec((None, CONV_W - 1, GDN_CONV_DIM), lambda b, n: (b, 0, 0)),
                  pl.BlockSpec((None, GDN_HV, GDN_DK, GDN_DV), lambda b, n: (b, 0, 0, 0)),
                  pl.BlockSpec((CONV_W, GDN_CONV_DIM), const2),
                  pl.BlockSpec((1, LANES), const2),
                  pl.BlockSpec((1, LANES), const2),
                  pl.BlockSpec((LANES, 1), const2),
                  pl.BlockSpec((LANES, 1), const2),
                  pl.BlockSpec((1, GDN_DV), const2)],
        out_specs=[pl.BlockSpec((c, GDN_VAL_DIM), lambda b, n: (b * nch + n, 0)),
                   pl.BlockSpec((None, CONV_W - 1, GDN_CONV_DIM), lambda b, n: (b, 0, 0)),
                   pl.BlockSpec((None, GDN_HV, GDN_DK, GDN_DV), lambda b, n: (b, 0, 0, 0))],
        out_shape=[jax.ShapeDtypeStruct((nb * nch * c, GDN_VAL_DIM), F32),
                   jax.ShapeDtypeStruct((nb, CONV_W - 1, GDN_CONV_DIM), F32),
                   jax.ShapeDtypeStruct((nb, GDN_HV, GDN_DK, GDN_DV), F32)],
        scratch_shapes=[pltpu.VMEM((HALO + c, GDN_CONV_DIM), F32),
                        pltpu.VMEM((c, GDN_CONV_DIM), F32),
                        pltpu.VMEM((GDN_HV, GDN_DK, GDN_DV), F32),
                        pltpu.VMEM((LANES, c), F32)],
        compiler_params=pltpu.CompilerParams(
            dimension_semantics=("arbitrary", "arbitrary"), vmem_limit_bytes=VMEM_LIMIT),
        name=f"gdn_c{c}",
    )(pm, ps, pst, conv_in, s_in, conv_w, alog_r, dtb_r, alog_r.reshape(LANES, 1), dtb_r.reshape(LANES, 1),
      g_norm_w.reshape(1, GDN_DV))


def _mlstm_kernel(pm_ref, ps_ref, pst_ref, c_in_ref, n_in_ref, m_in_ref, gb_r_ref, gb_c_ref, nw_ref,
                  out_ref, c_out_ref, n_out_ref, m_out_ref,
                  c_s, n_s, m_s, *, c):
    step = pl.program_id(1)

    @pl.when(step == 0)
    def _():
        c_s[...] = c_in_ref[...]
        n_s[...] = n_in_ref[...]
        m_s[...] = m_in_ref[...]

    row, col = _iota2(c)
    causal = row >= col
    pre_r = _softcap(ps_ref[...] + gb_r_ref[...])
    bcum_all = _dot(jnp.where(causal, 1.0, 0.0).astype(F32), _log_sigmoid(pre_r), HP)
    pre_t = _softcap(pst_ref[...] + gb_c_ref[...])
    bcum_t = _dot(_log_sigmoid(pre_t), jnp.where(row <= col, 1.0, 0.0).astype(F32), HP)

    for h in range(ML_H):
        q = pm_ref[:, h * ML_DQK:(h + 1) * ML_DQK] * (ML_DQK ** -0.5)
        k = pm_ref[:, ML_QK_DIM + h * ML_DQK:ML_QK_DIM + (h + 1) * ML_DQK]
        v = pm_ref[:, 2 * ML_QK_DIM + h * ML_DV:2 * ML_QK_DIM + (h + 1) * ML_DV]
        og = pm_ref[:, 2 * ML_QK_DIM + ML_V_DIM + h * ML_DV:2 * ML_QK_DIM + ML_V_DIM + (h + 1) * ML_DV]
        li_c = pre_r[:, LANE_I + h:LANE_I + h + 1]
        b_c = bcum_all[:, LANE_F + h:LANE_F + h + 1]
        li_r = pre_t[LANE_I + h:LANE_I + h + 1, :]
        b_r = bcum_t[LANE_F + h:LANE_F + h + 1, :]
        m_prev = m_s[:, h:h + 1]
        dlog = jnp.where(causal, b_c - b_r + li_r, -jnp.inf)
        dmax = jnp.max(dlog, axis=-1, keepdims=True)
        inter = b_c + m_prev
        mt = jnp.maximum(inter, dmax)
        qb = q.astype(BF16)
        kb16 = k.astype(BF16)
        vb = v.astype(BF16)
        pw = jnp.exp(dlog - mt) * _dot_nt(qb, kb16)
        sc_in = jnp.exp(inter - mt)
        cs = c_s[h]
        num = sc_in * _dot(qb, cs.astype(BF16)) + _dot(pw.astype(BF16), vb)
        den = sc_in * jnp.sum(q * n_s[h], axis=-1, keepdims=True) + jnp.sum(pw, axis=-1, keepdims=True)
        hh = num / jnp.maximum(jnp.abs(den), jnp.exp(-mt))
        b_last = b_r[:, c - 1:c]
        w_c = b_last - b_c + li_c
        w_r = b_last - b_r + li_r
        m_new = jnp.maximum(b_last + m_prev, jnp.max(w_r, axis=-1, keepdims=True))
        sc = jnp.exp(b_last + m_prev - m_new)
        kw = k * jnp.exp(w_c - m_new)
        c_s[h] = sc * cs + _dot_tn(kw.astype(BF16), vb)
        n_s[h] = sc * n_s[h] + jnp.sum(kw, axis=0, keepdims=True)
        m_s[:, h:h + 1] = m_new
        hn = hh * lax.rsqrt(jnp.mean(hh * hh, axis=-1, keepdims=True) + RMS_EPS) * nw_ref[:, h * ML_DV:(h + 1) * ML_DV]
        out_ref[:, h * ML_DV:(h + 1) * ML_DV] = hn * jax.nn.sigmoid(og)

    @pl.when(step == pl.num_programs(1) - 1)
    def _():
        c_out_ref[...] = c_s[...]
        n_out_ref[...] = n_s[...]
        m_out_ref[...] = m_s[...]


def _mlstm_mixer(pm, ps, pst, c_in, n_in, m_in, gate_bias, norm_w, *, row0, nb, nch, c):
    assert row0 % c == 0
    blk0 = row0 // c
    gb_r = (jnp.zeros((1, LANES), F32).at[0, LANE_I:LANE_I + ML_H].set(gate_bias[0])
            .at[0, LANE_F:LANE_F + ML_H].set(gate_bias[1]))
    rows = lambda b, n: (blk0 + b * nch + n, 0)
    const2 = lambda b, n: (0, 0)
    out, c_out, n_out, m_out = pl.pallas_call(
        functools.partial(_mlstm_kernel, c=c),
        grid=(nb, nch),
        in_specs=[pl.BlockSpec((c, ML_COLS), lambda b, n: (blk0 + b * nch + n, 1)),
                  pl.BlockSpec((c, LANES), rows),
                  pl.BlockSpec((None, None, LANES, c), lambda b, n: (b, n, 0, 0)),
                  pl.BlockSpec((None, ML_H, ML_DQK, ML_DV), lambda b, n: (b, 0, 0, 0)),
                  pl.BlockSpec((None, ML_H, 1, ML_DQK), lambda b, n: (b, 0, 0, 0)),
                  pl.BlockSpec((None, 1, ML_H), lambda b, n: (b, 0, 0)),
                  pl.BlockSpec((1, LANES), const2),
                  pl.BlockSpec((LANES, 1), const2),
                  pl.BlockSpec((1, ML_V_DIM), const2)],
        out_specs=[pl.BlockSpec((c, ML_V_DIM), lambda b, n: (b * nch + n, 0)),
                   pl.BlockSpec((None, ML_H, ML_DQK, ML_DV), lambda b, n: (b, 0, 0, 0)),
                   pl.BlockSpec((None, ML_H, 1, ML_DQK), lambda b, n: (b, 0, 0, 0)),
                   pl.BlockSpec((None, 1, ML_H), lambda b, n: (b, 0, 0))],
        out_shape=[jax.ShapeDtypeStruct((nb * nch * c, ML_V_DIM), F32),
                   jax.ShapeDtypeStruct((nb, ML_H, ML_DQK, ML_DV), F32),
                   jax.ShapeDtypeStruct((nb, ML_H, 1, ML_DQK), F32),
                   jax.ShapeDtypeStruct((nb, 1, ML_H), F32)],
        scratch_shapes=[pltpu.VMEM((ML_H, ML_DQK, ML_DV), F32),
                        pltpu.VMEM((ML_H, 1, ML_DQK), F32),
                        pltpu.VMEM((1, ML_H), F32)],
        compiler_params=pltpu.CompilerParams(
            dimension_semantics=("arbitrary", "arbitrary"), vmem_limit_bytes=VMEM_LIMIT),
        name=f"mlstm_c{c}",
    )(pm, ps, pst, c_in, n_in.reshape(nb, ML_H, 1, ML_DQK), m_in.reshape(nb, 1, ML_H),
      gb_r, gb_r.reshape(LANES, 1), norm_w.reshape(1, ML_V_DIM))
    return out, c_out, n_out.reshape(nb, ML_H, ML_DQK), m_out.reshape(nb, ML_H)


def _mix_rows(pm, ps, state, p, *, row0, nb, nch, c):
    conv_w, a_log, dt_bias, g_norm_w, ml_gb, ml_nw = p
    conv_in, s_in, c_in, n_in, m_in = state
    n_rows = nb * nch * c
    pst = jnp.swapaxes(ps[row0:row0 + n_rows].reshape(nb, nch, c, LANES), 2, 3)
    og, conv_out, s_out = _gdn_mixer(pm, ps, pst, conv_in, s_in, conv_w, a_log, dt_bias, g_norm_w,
                                     row0=row0, nb=nb, nch=nch, c=c)
    om, c_out, n_out, m_out = _mlstm_mixer(pm, ps, pst, c_in, n_in, m_in, ml_gb, ml_nw,
                                           row0=row0, nb=nb, nch=nch, c=c)
    return jnp.concatenate([og, om], axis=1), (conv_out, s_out, c_out, n_out, m_out)


def kernel(x_prompt, x_sample, state_gdn_conv, state_gdn, state_mlstm_c, state_mlstm_n, state_mlstm_m,
           meta_tokens, norm_mix_w, w_in, gdn_conv_w, gdn_a_log, gdn_dt_bias, gdn_norm_w,
           ml_gate_bias, ml_norm_w, w_out, norm_ffn_w, router_w, router_b,
           w_gate, b_gate, w_up, b_up, w_down, b_down, norm_final_w):
    B, S, D = x_prompt.shape
    BS, TS, _ = x_sample.shape
    n_main = B * S
    n_meta = B * N_META
    row_meta = n_main
    row_samp = n_main + n_meta
    n_tok = row_samp + BS * TS
    meta = jnp.broadcast_to(meta_tokens.astype(x_prompt.dtype), (B, N_META, D))
    x_all = jnp.concatenate([x_prompt.reshape(n_main, D), meta.reshape(n_meta, D),
                             x_sample.reshape(BS * TS, D)], axis=0)

    w = w_in[0]
    offs = [0]
    for s in IN_SPLITS:
        offs.append(offs[-1] + s)
    col = lambda i: w[:, offs[i]:offs[i + 1]]
    w_main = jnp.concatenate([col(0), col(1), col(2), col(3), col(6), col(7), col(8), col(9)], axis=1)
    w_small = jnp.concatenate([col(4), col(5), col(10), col(11)], axis=1)
    w_small = jnp.pad(w_small, ((0, 0), (0, LANES - w_small.shape[1])))

    tm = n_tok // 10
    h = _rmsnorm(x_all, norm_mix_w[0], BF16, n_tok // 20)
    proj_main = _matmul(h, w_main, tm, 512)
    proj_small = _matmul(h, w_small, tm, LANES)

    mix_p = (gdn_conv_w[0], gdn_a_log[0], gdn_dt_bias[0], gdn_norm_w[0], ml_gate_bias[0], ml_norm_w[0])
    p_init = (jnp.zeros((B, CONV_W - 1, GDN_CONV_DIM), F32),
              jnp.zeros((B, GDN_HV, GDN_DK, GDN_DV), F32),
              jnp.zeros((B, ML_H, ML_DQK, ML_DV), F32),
              jnp.zeros((B, ML_H, ML_DQK), F32),
              jnp.zeros((B, ML_H), F32))
    mixed_meta, st = _mix_rows(proj_main, proj_small, p_init, mix_p, row0=row_meta, nb=B, nch=1, c=N_META)
    c_main = math.gcd(S, GDN_CHUNK)
    mixed_main, p_st = _mix_rows(proj_main, proj_small, st, mix_p, row0=0, nb=B, nch=S // c_main, c=c_main)
    s_init = (state_gdn_conv[0], state_gdn[0], state_mlstm_c[0], state_mlstm_n[0], state_mlstm_m[0])
    c_samp = math.gcd(TS, GDN_CHUNK)
    mixed_samp, s_st = _mix_rows(proj_main, proj_small, s_init, mix_p, row0=row_samp, nb=BS,
                                 nch=TS // c_samp, c=c_samp)
    mixed = jnp.concatenate([mixed_main, mixed_meta, mixed_samp], axis=0).astype(BF16)

    x1 = _matmul(mixed, w_out[0], tm, 512, residual=x_all)

    h2, gates_l, idx_l = _router(x1, norm_ffn_w[0], router_w[0], router_b[0], n_tok // 20)
    rank_l, cnt = _expert_ranks(idx_l, n_tok // 20)
    counts = cnt[0, :N_EXPERTS]
    padded = (counts + MOE_TM - 1) // MOE_TM * MOE_TM
    pad_end = jnp.cumsum(padded)
    pad_start = pad_end - padded
    n_assign = n_tok * TOP_K
    dest = (pad_start[idx_l[:, :TOP_K]] + rank_l[:, :TOP_K]).reshape(n_assign)
    n_blocks = -(-n_assign // MOE_TM) + N_EXPERTS
    n_rows = n_blocks * MOE_TM
    block_lo = jnp.arange(n_blocks, dtype=jnp.int32) * MOE_TM
    block_e = jnp.minimum(jnp.sum((pad_end[None, :] <= block_lo[:, None]).astype(jnp.int32), axis=1),
                          N_EXPERTS - 1)
    n_valid = pad_end[-1] // MOE_TM
    block_rows = jnp.clip((pad_start + counts)[block_e] - block_lo, 0, MOE_TM)
    xs = _dispatch_rows(h2, dest, n_rows, DISPATCH_TB)
    y = _moe_experts(xs, block_e, n_valid, block_rows, w_gate[0], b_gate[0], w_up[0], b_up[0],
                     w_down[0], b_down[0], 512, 1024)
    y_all = _combine_norm(x1, gates_l, dest, y, norm_final_w, COMBINE_TB)
    y_prompt = y_all[:n_main].reshape(B, S, D)
    y_sample = y_all[row_samp:].reshape(BS, TS, D)

    likes = (state_gdn_conv, state_gdn, state_mlstm_c, state_mlstm_n, state_mlstm_m)
    p_out = tuple(a[None].astype(l.dtype) for a, l in zip(p_st, likes))
    s_out = tuple(a[None].astype(l.dtype) for a, l in zip(s_st, likes))
    return (y_prompt, y_sample) + p_out + s_out
```
